```python
import math
import jax, jax.numpy as jnp
from jax import lax
import numpy as np

D_MODEL = 4096
BATCH = 4
SEQ = 2048
DEPTH = 2

CTX_LEN = 256
GRID_W = 64
HEAD_DIM = 128
ROPE_THETA = 10000.0
EPS = 1e-6
Q_BLOCK = 128

GLA_DV = 2 * HEAD_DIM
GLA_DK = HEAD_DIM
GLA_HEADS = (D_MODEL // 4) // GLA_DV
GLA_RANK = 16
GLA_TAU = 16.0
GLA_CHUNK = 64
GQA_HEADS = (D_MODEL // 2) // HEAD_DIM
GQA_KV_HEADS = GQA_HEADS // 4
GQA_GROUP = GQA_HEADS // GQA_KV_HEADS
DIFF_DV = 2 * HEAD_DIM
DIFF_HEADS = (D_MODEL // 4) // DIFF_DV

GLA_WIDTH = GLA_HEADS * GLA_DV
GQA_WIDTH = GQA_HEADS * HEAD_DIM
DIFF_WIDTH = DIFF_HEADS * DIFF_DV
MIX_WIDTH = GLA_WIDTH + GQA_WIDTH + DIFF_WIDTH

IN_SPLITS = (
    GLA_HEADS * GLA_DK, GLA_HEADS * GLA_DK, GLA_HEADS * GLA_DV, GLA_HEADS * GLA_DV, GLA_RANK, GLA_RANK,
    GQA_HEADS * HEAD_DIM, GQA_KV_HEADS * HEAD_DIM, GQA_KV_HEADS * HEAD_DIM,
    DIFF_HEADS * 2 * HEAD_DIM, DIFF_HEADS * 2 * HEAD_DIM, DIFF_HEADS * DIFF_DV,
)
IN_WIDTH = sum(IN_SPLITS)

N_GROUPS = 4
EXPERTS_PER_GROUP = 8
TOP_K_EXPERTS = 2
EXPERT_FF = (3 * D_MODEL) // 16

kernel_name = 'hymba_style_hybrid_dit_block'


def rmsnorm(x, g):
    xf = x.astype(jnp.float32)
    y = xf * lax.rsqrt(jnp.mean(xf * xf, axis=-1, keepdims=True) + EPS)
    return (y * g.astype(jnp.float32)).astype(x.dtype)


def to_heads(t, h, d):
    b, n = t.shape[0], t.shape[1]
    return t.reshape(b, n, h, d).transpose(0, 2, 1, 3)


def from_heads(t):
    b, h, n, d = t.shape
    return t.transpose(0, 2, 1, 3).reshape(b, n, h * d)


def split_projection(p):
    offs, acc = [], 0
    for w in IN_SPLITS[:-1]:
        acc += w
        offs.append(acc)
    return jnp.split(p, offs, axis=-1)


def axial_rope_tables(row, col):
    nf = HEAD_DIM // 4
    inv = ROPE_THETA ** (-jnp.arange(nf, dtype=jnp.float32) / nf)
    ang = jnp.stack([row.astype(jnp.float32)[:, None] * inv, col.astype(jnp.float32)[:, None] * inv], axis=1)
    return jnp.cos(ang), jnp.sin(ang)


def apply_axial_rope(t, cos, sin):
    nf = HEAD_DIM // 4
    tf = t.astype(jnp.float32).reshape(t.shape[:-1] + (2, 2, nf))
    t1, t2 = tf[..., 0, :], tf[..., 1, :]
    out = jnp.stack([t1 * cos - t2 * sin, t2 * cos + t1 * sin], axis=-2)
    return out.reshape(t.shape).astype(t.dtype)


def sweep_query_blocks(fn, q):
    n = q.shape[-2]
    nb = n // Q_BLOCK
    qb = jnp.moveaxis(q.reshape(q.shape[:-2] + (nb, Q_BLOCK, q.shape[-1])), -3, 0)
    out = jnp.moveaxis(lax.map(fn, qb), 0, -3)
    return out.reshape(out.shape[:-3] + (n, out.shape[-1]))


def gla_prep(pq, pk, pv, paf, pab, wa_f, ba_f, wa_b, ba_b):
    q = to_heads(pq, GLA_HEADS, GLA_DK) * (GLA_DK ** -0.5)
    k = to_heads(pk, GLA_HEADS, GLA_DK)
    v = to_heads(pv, GLA_HEADS, GLA_DV)
    la_f = jax.nn.log_sigmoid((paf @ wa_f + ba_f).astype(jnp.float32)) / GLA_TAU
    la_b = jax.nn.log_sigmoid((pab @ wa_b + ba_b).astype(jnp.float32)) / GLA_TAU
    return q, k, v, to_heads(la_f, GLA_HEADS, GLA_DK), to_heads(la_b, GLA_HEADS, GLA_DK)


def gla_chunk_scan(q, k, v, log_a, s0):
    bq, h, n, _ = q.shape
    dv = v.shape[-1]
    nc = n // GLA_CHUNK

    def chunks(t):
        t = t.astype(jnp.float32).reshape(bq, h, nc, GLA_CHUNK, t.shape[-1])
        return jnp.moveaxis(t, 2, 0)

    incl = jnp.tril(jnp.ones((GLA_CHUNK, GLA_CHUNK), dtype=bool))[:, :, None]

    def step(s, inp):
        qc, kc, vc, ac = inp
        b = jnp.cumsum(ac, axis=-2)
        rel = b[:, :, :, None, :] - b[:, :, None, :, :]
        decay = jnp.exp(jnp.where(incl, rel, -jnp.inf))
        scores = jnp.einsum('bhik,bhjk,bhijk->bhij', qc, kc, decay)
        o = jnp.einsum('bhij,bhjv->bhiv', scores, vc) + jnp.einsum('bhik,bhkv->bhiv', qc * jnp.exp(b), s)
        b_end = b[:, :, -1:, :]
        s_new = jnp.exp(b_end[:, :, 0, :])[..., None] * s + jnp.einsum('bhjk,bhjv->bhkv', kc * jnp.exp(b_end - b), vc)
        return s_new, o

    s_fin, o = lax.scan(step, s0, (chunks(q), chunks(k), chunks(v), chunks(log_a)))
    o = jnp.moveaxis(o, 0, 2).reshape(bq, h, n, dv)
    return o.astype(v.dtype), s_fin


def gla_bidir(q, k, v, la_f, la_b, s0_f, s0_b):
    o_f, s_f = gla_chunk_scan(q, k, v, la_f, s0_f)
    flip = lambda t: jnp.flip(t, axis=2)
    o_b, s_b = gla_chunk_scan(flip(q), flip(k), flip(v), flip(la_b), s0_b)
    return o_f + flip(o_b), s_f, s_b


def gla_output(o, pg, norm_g):
    b, h, n, dv = o.shape
    on = rmsnorm(o.transpose(0, 2, 1, 3), norm_g)
    return (on * jax.nn.silu(pg.reshape(b, n, h, dv))).reshape(b, n, h * dv)


def gqa_prep(pq, pk, pv, qn, kn):
    q = rmsnorm(to_heads(pq, GQA_HEADS, HEAD_DIM), qn)
    k = rmsnorm(to_heads(pk, GQA_KV_HEADS, HEAD_DIM), kn)
    v = to_heads(pv, GQA_KV_HEADS, HEAD_DIM)
    return q, k, v


def gqa_attend(q, k, v):
    b, hq, n, dh = q.shape
    qg = q.reshape(b, GQA_KV_HEADS, GQA_GROUP, n, dh)
    scale = HEAD_DIM ** -0.5

    def block(qb):
        s = jnp.einsum('bkgqd,bksd->bkgqs', qb, k).astype(jnp.float32) * scale
        p = jax.nn.softmax(s, axis=-1)
        return jnp.einsum('bkgqs,bksd->bkgqd', p.astype(v.dtype), v)

    return from_heads(sweep_query_blocks(block, qg).reshape(b, hq, n, dh))


def diff_prep(pq, pk, pv):
    b, n = pq.shape[0], pq.shape[1]
    q = pq.reshape(b, n, DIFF_HEADS, 2, HEAD_DIM).transpose(0, 2, 3, 1, 4)
    k = pk.reshape(b, n, DIFF_HEADS, 2, HEAD_DIM).transpose(0, 2, 3, 1, 4)
    v = to_heads(pv, DIFF_HEADS, DIFF_DV)
    return q, k, v


def diff_attend(q, k, v, lam, norm_g, lam_init):
    scale = HEAD_DIM ** -0.5

    def block(qb):
        s = jnp.einsum('bhiqd,bhisd->bhiqs', qb, k).astype(jnp.float32) * scale
        p = jax.nn.softmax(s, axis=-1)
        w = p[:, :, 0] - lam * p[:, :, 1]
        return jnp.einsum('bhqs,bhsd->bhqd', w.astype(v.dtype), v)

    o = sweep_query_blocks(block, q)
    return from_heads(rmsnorm(o, norm_g) * (1.0 - lam_init))


def hier_moe(h, router_wg, router_bg, router_we, router_be, w_gate, w_up, w_down):
    shp = h.shape
    t = h.reshape(-1, shp[-1])
    g_prob = jax.nn.softmax((t @ router_wg).astype(jnp.float32) + router_bg, axis=-1)
    g_w, g_idx = lax.top_k(g_prob, 1)
    g_onehot = jax.nn.one_hot(g_idx[:, 0], N_GROUPS, dtype=jnp.float32)
    e_logits = ((t @ router_we).astype(jnp.float32) + router_be).reshape(-1, N_GROUPS, EXPERTS_PER_GROUP)
    e_prob = jax.nn.softmax(jnp.einsum('tg,tge->te', g_onehot, e_logits), axis=-1)
    e_w, e_idx = lax.top_k(e_prob, TOP_K_EXPERTS)
    e_w = e_w / jnp.sum(e_w, axis=-1, keepdims=True)
    w_e = jnp.einsum('tk,tke->te', e_w, jax.nn.one_hot(e_idx, EXPERTS_PER_GROUP, dtype=jnp.float32))
    combine = (g_onehot[:, :, None] * (g_w * w_e)[:, None, :]).astype(t.dtype)
    out = jnp.zeros_like(t)
    for gi in range(N_GROUPS):
        a = jnp.einsum('td,edf->tef', t, w_gate[gi])
        u = jnp.einsum('td,edf->tef', t, w_up[gi])
        hid = jax.nn.silu(a) * u * combine[:, gi, :, None]
        out = out + jnp.einsum('tef,efd->td', hid, w_down[gi])
    return out.reshape(shp)


def hybrid_layer(x, cx, c, c_ctx, cos, sin, lam_init, need_ctx,
                 ada_w, ada_b, norm1_g, w_in, gla_wa_fwd, gla_ba_fwd, gla_wa_bwd, gla_ba_bwd, gla_norm_g,
                 gqa_qnorm_g, gqa_knorm_g, diff_lq1, diff_lk1, diff_lq2, diff_lk2, diff_norm_g, w_out,
                 norm2_g, router_wg, router_bg, router_we, router_be, moe_w_gate, moe_w_up, moe_w_down):
    sh1, sc1, g1, sh2, sc2, g2 = [m[:, None, :] for m in jnp.split(jax.nn.silu(c) @ ada_w + ada_b, 6, axis=-1)]
    csh1, csc1, cg1, csh2, csc2, cg2 = jnp.split(jax.nn.silu(c_ctx) @ ada_w + ada_b, 6, axis=-1)

    h_l = rmsnorm(x, norm1_g) * (1.0 + sc1) + sh1
    h_c = rmsnorm(cx, norm1_g) * (1.0 + csc1) + csh1
    (aq_l, ak_l, av_l, ag_l, aaf_l, aab_l, bq_l, bk_l, bv_l, dq_l, dk_l, dv_l) = split_projection(h_l @ w_in)
    (aq_c, ak_c, av_c, ag_c, aaf_c, aab_c, bq_c, bk_c, bv_c, dq_c, dk_c, dv_c) = split_projection(h_c @ w_in)

    qa_c, ka_c, va_c, laf_c, lab_c = gla_prep(aq_c, ak_c, av_c, aaf_c, aab_c, gla_wa_fwd, gla_ba_fwd, gla_wa_bwd, gla_ba_bwd)
    qa_l, ka_l, va_l, laf_l, lab_l = gla_prep(aq_l, ak_l, av_l, aaf_l, aab_l, gla_wa_fwd, gla_ba_fwd, gla_wa_bwd, gla_ba_bwd)
    s_zero = jnp.zeros((cx.shape[0], GLA_HEADS, GLA_DK, GLA_DV), jnp.float32)
    oa_c, s_f, s_b = gla_bidir(qa_c, ka_c, va_c, laf_c, lab_c, s_zero, s_zero)
    oa_l, _, _ = gla_bidir(qa_l, ka_l, va_l, laf_l, lab_l, s_f, s_b)
    mix_a_l = gla_output(oa_l, ag_l, gla_norm_g)

    qb_c, kb_c, vb_c = gqa_prep(bq_c, bk_c, bv_c, gqa_qnorm_g, gqa_knorm_g)
    qb_l, kb_l, vb_l = gqa_prep(bq_l, bk_l, bv_l, gqa_qnorm_g, gqa_knorm_g)
    qb_l = apply_axial_rope(qb_l, cos, sin)
    kb_l = apply_axial_rope(kb_l, cos, sin)
    mix_b_l = gqa_attend(qb_l, jnp.concatenate([kb_c, kb_l], axis=2), jnp.concatenate([vb_c, vb_l], axis=2))

    lam = (jnp.exp(jnp.sum(diff_lq1.astype(jnp.float32) * diff_lk1.astype(jnp.float32)))
           - jnp.exp(jnp.sum(diff_lq2.astype(jnp.float32) * diff_lk2.astype(jnp.float32))) + lam_init)
    qd_c, kd_c, vd_c = diff_prep(dq_c, dk_c, dv_c)
    qd_l, kd_l, vd_l = diff_prep(dq_l, dk_l, dv_l)
    qd_l = apply_axial_rope(qd_l, cos, sin)
    kd_l = apply_axial_rope(kd_l, cos, sin)
    mix_c_l = diff_attend(qd_l, jnp.concatenate([kd_c, kd_l], axis=3), jnp.concatenate([vd_c, vd_l], axis=2),
                          lam, diff_norm_g, lam_init)

    x = x + g1 * (jnp.concatenate([mix_a_l, mix_b_l, mix_c_l], axis=-1) @ w_out)
    x = x + g2 * hier_moe(rmsnorm(x, norm2_g) * (1.0 + sc2) + sh2,
                          router_wg, router_bg, router_we, router_be, moe_w_gate, moe_w_up, moe_w_down)

    if need_ctx:
        mix_a_c = gla_output(oa_c, ag_c, gla_norm_g)
        mix_b_c = gqa_attend(qb_c, kb_c, vb_c)
        mix_c_c = diff_attend(qd_c, kd_c, vd_c, lam, diff_norm_g, lam_init)
        cx = cx + cg1 * (jnp.concatenate([mix_a_c, mix_b_c, mix_c_c], axis=-1) @ w_out)
        cx = cx + cg2 * hier_moe(rmsnorm(cx, norm2_g) * (1.0 + csc2) + csh2,
                                 router_wg, router_bg, router_we, router_be, moe_w_gate, moe_w_up, moe_w_down)
    return x, cx


def setup_inputs(seed: int = 0) -> dict:
    key = jax.random.key(seed)
    ks = jax.random.split(key, 32)
    D = D_MODEL
    nrm = lambda k, shape, s: jax.random.normal(k, shape, jnp.float32) * s
    gain = lambda k, shape: 1.0 + 0.02 * jax.random.normal(k, shape, jnp.float32)
    G, E, F = N_GROUPS, EXPERTS_PER_GROUP, EXPERT_FF
    return {
        'x': nrm(ks[0], (BATCH, SEQ, D), 1.0),
        'c': nrm(ks[1], (BATCH, D), 1.0),
        'ctx': nrm(ks[2], (BATCH, CTX_LEN, D), 1.0),
        'c_ctx': nrm(ks[3], (D,), 1.0),
        'ada_w': nrm(ks[4], (DEPTH, D, 6 * D), 0.5 * D ** -0.5),
        'ada_b': nrm(ks[5], (DEPTH, 6 * D), 0.01),
        'norm1_g': gain(ks[6], (DEPTH, D)),
        'w_in': nrm(ks[7], (DEPTH, D, IN_WIDTH), D ** -0.5),
        'gla_wa_fwd': nrm(ks[8], (DEPTH, GLA_RANK, GLA_HEADS * GLA_DK), GLA_RANK ** -0.5),
        'gla_ba_fwd': nrm(ks[9], (DEPTH, GLA_HEADS * GLA_DK), 0.5),
        'gla_wa_bwd': nrm(ks[10], (DEPTH, GLA_RANK, GLA_HEADS * GLA_DK), GLA_RANK ** -0.5),
        'gla_ba_bwd': nrm(ks[11], (DEPTH, GLA_HEADS * GLA_DK), 0.5),
        'gla_norm_g': gain(ks[12], (DEPTH, GLA_DV)),
        'gqa_qnorm_g': gain(ks[13], (DEPTH, HEAD_DIM)),
        'gqa_knorm_g': gain(ks[14], (DEPTH, HEAD_DIM)),
        'diff_lq1': nrm(ks[15], (DEPTH, HEAD_DIM), 0.1),
        'diff_lk1': nrm(ks[16], (DEPTH, HEAD_DIM), 0.1),
        'diff_lq2': nrm(ks[17], (DEPTH, HEAD_DIM), 0.1),
        'diff_lk2': nrm(ks[18], (DEPTH, HEAD_DIM), 0.1),
        'diff_norm_g': gain(ks[19], (DEPTH, DIFF_DV)),
        'w_out': nrm(ks[20], (DEPTH, MIX_WIDTH, D), MIX_WIDTH ** -0.5),
        'norm2_g': gain(ks[21], (DEPTH, D)),
        'router_wg': nrm(ks[22], (DEPTH, D, G), D ** -0.5),
        'router_bg': nrm(ks[23], (DEPTH, G), 0.01),
        'router_we': nrm(ks[24], (DEPTH, D, G * E), D ** -0.5),
        'router_be': nrm(ks[25], (DEPTH, G * E), 0.01),
        'moe_w_gate': nrm(ks[26], (DEPTH, G, E, D, F), D ** -0.5),
        'moe_w_up': nrm(ks[27], (DEPTH, G, E, D, F), D ** -0.5),
        'moe_w_down': nrm(ks[28], (DEPTH, G, E, F, D), F ** -0.5),
        'final_norm_g': gain(ks[29], (D,)),
    }


def reference(x, c, ctx, c_ctx, ada_w, ada_b, norm1_g, w_in, gla_wa_fwd, gla_ba_fwd, gla_wa_bwd, gla_ba_bwd,
              gla_norm_g, gqa_qnorm_g, gqa_knorm_g, diff_lq1, diff_lk1, diff_lq2, diff_lk2, diff_norm_g, w_out,
              norm2_g, router_wg, router_bg, router_we, router_be, moe_w_gate, moe_w_up, moe_w_down, final_norm_g):
    ROWS = x.shape[1] // GRID_W
    row = jnp.repeat(jnp.arange(ROWS, dtype=jnp.int32), GRID_W)
    col = jnp.tile(jnp.arange(GRID_W, dtype=jnp.int32), ROWS)
    cos, sin = axial_rope_tables(row, col)
    cx = ctx
    for l in range(DEPTH):
        lam_init = 0.8 - 0.6 * math.exp(-0.3 * l)
        x, cx = hybrid_layer(
            x, cx, c, c_ctx, cos, sin, lam_init, l < DEPTH - 1,
            ada_w[l], ada_b[l], norm1_g[l], w_in[l], gla_wa_fwd[l], gla_ba_fwd[l], gla_wa_bwd[l], gla_ba_bwd[l],
            gla_norm_g[l], gqa_qnorm_g[l], gqa_knorm_g[l], diff_lq1[l], diff_lk1[l], diff_lq2[l], diff_lk2[l],
            diff_norm_g[l], w_out[l], norm2_g[l], router_wg[l], router_bg[l], router_we[l], router_be[l],
            moe_w_gate[l], moe_w_up[l], moe_w_down[l])
    return rmsnorm(x, final_norm_g)
```

```python
import functools
import math

import jax
import jax.numpy as jnp
from jax import lax
from jax.experimental import pallas as pl
from jax.experimental.pallas import tpu as pltpu

F32 = jnp.float32
BF16 = jnp.bfloat16

HEAD_DIM = 128
GRID_W = 64
ROPE_THETA = 10000.0
EPS = 1e-6
GLA_DK = HEAD_DIM
GLA_DV = 2 * HEAD_DIM
GLA_RANK = 16
GLA_TAU = 16.0
GLA_CHUNK = 64
GQA_GROUP = 4
DIFF_DV = 2 * HEAD_DIM
N_GROUPS = 4
EXPERTS_PER_GROUP = 8
N_EXPERTS = N_GROUPS * EXPERTS_PER_GROUP

LANES = 128
ROW_TILE = 256
MOE_TILE = 256
VMEM_LIMIT = 56 * 1024 * 1024


def _cparams(sem, vmem=VMEM_LIMIT):
    return pltpu.CompilerParams(dimension_semantics=sem, vmem_limit_bytes=vmem)


def _sigmoid(x):
    return 1.0 / (1.0 + jnp.exp(-x))


def _silu(x):
    return x * _sigmoid(x)


def _log_sigmoid(x):
    return jnp.minimum(x, 0.0) - jnp.log(1.0 + jnp.exp(-jnp.abs(x)))


def _rms(x, g):
    return x * lax.rsqrt(jnp.mean(x * x, axis=-1, keepdims=True) + EPS) * g


def _dot(a, b):
    return jnp.dot(a, b, preferred_element_type=F32)


def _dot_nt(a, b):
    return lax.dot_general(a, b, (((1,), (1,)), ((), ())), preferred_element_type=F32)


def _dot_tn(a, b):
    return lax.dot_general(a, b, (((0,), (0,)), ((), ())), preferred_element_type=F32)


def _ada_kernel(c_ref, w_ref, b_ref, o_ref):
    s = _silu(c_ref[...]).astype(BF16)
    o_ref[0] = _dot(s, w_ref[0].astype(BF16)) + b_ref[0]


def ada_modulation(c_all, ada_w, ada_b, *, tn=512):
    nl, d, n = ada_w.shape
    return pl.pallas_call(
        _ada_kernel,
        out_shape=jax.ShapeDtypeStruct((nl, 8, n), F32),
        grid=(nl, n // tn),
        in_specs=[
            pl.BlockSpec((8, d), lambda l, j: (0, 0)),
            pl.BlockSpec((1, d, tn), lambda l, j: (l, 0, j)),
            pl.BlockSpec((1, 1, tn), lambda l, j: (l, 0, j)),
        ],
        out_specs=pl.BlockSpec((1, 8, tn), lambda l, j: (l, 0, j)),
        compiler_params=_cparams(("arbitrary", "arbitrary")),
        name="ada_modulation",
    )(c_all, ada_w, ada_b.reshape(nl, 1, n))


def _norm_kernel(x_ref, g_ref, sc_ref, sh_ref, w_ref, h_ref, s_ref, *, exact_small):
    h = _rms(x_ref[...], g_ref[...]) * (1.0 + sc_ref[0]) + sh_ref[0]
    h_ref[...] = h.astype(h_ref.dtype)
    if exact_small:
        s_ref[...] = jnp.dot(h, w_ref[...], precision=lax.Precision.HIGHEST, preferred_element_type=F32)
    else:
        s_ref[...] = _dot(h.astype(BF16), w_ref[...].astype(BF16))


def norm_modulate(x, g, mod3, sc_chunk, sh_chunk, w_small, mod_row, *, h_dtype, exact_small):
    t, d = x.shape
    ns = w_small.shape[1]
    return pl.pallas_call(
        functools.partial(_norm_kernel, exact_small=exact_small),
        out_shape=(jax.ShapeDtypeStruct((t, d), h_dtype), jax.ShapeDtypeStruct((t, ns), F32)),
        grid=(t // ROW_TILE,),
        in_specs=[
            pl.BlockSpec((ROW_TILE, d), lambda i: (i, 0)),
            pl.BlockSpec((1, d), lambda i: (0, 0)),
            pl.BlockSpec((1, 1, d), lambda i: (mod_row(i), 0, sc_chunk)),
            pl.BlockSpec((1, 1, d), lambda i: (mod_row(i), 0, sh_chunk)),
            pl.BlockSpec((d, ns), lambda i: (0, 0)),
        ],
        out_specs=(pl.BlockSpec((ROW_TILE, d), lambda i: (i, 0)), pl.BlockSpec((ROW_TILE, ns), lambda i: (i, 0))),
        compiler_params=_cparams(("arbitrary",)),
        name="norm_modulate",
    )(x, g.reshape(1, d), mod3, mod3, w_small)


def _final_norm_kernel(x_ref, g_ref, o_ref):
    o_ref[...] = _rms(x_ref[...], g_ref[...])


def final_norm(x, g, row_block):
    d = x.shape[1]
    n_out = row_block.n_out
    return pl.pallas_call(
        _final_norm_kernel,
        out_shape=jax.ShapeDtypeStruct((n_out * ROW_TILE, d), F32),
        grid=(n_out,),
        in_specs=[pl.BlockSpec((ROW_TILE, d), lambda i: (row_block(i), 0)), pl.BlockSpec((1, d), lambda i: (0, 0))],
        out_specs=pl.BlockSpec((ROW_TILE, d), lambda i: (i, 0)),
        compiler_params=_cparams(("arbitrary",)),
        name="final_norm",
    )(x, g.reshape(1, d))


def _mm_kernel(a_ref, b_ref, o_ref):
    o_ref[...] = _dot(a_ref[...], b_ref[...]).astype(o_ref.dtype)


def matmul(a, b, *, bm, bn, out_dtype):
    m, k = a.shape
    n = b.shape[1]
    return pl.pallas_call(
        _mm_kernel,
        out_shape=jax.ShapeDtypeStruct((m, n), out_dtype),
        grid=(m // bm, n // bn),
        in_specs=[pl.BlockSpec((bm, k), lambda i, j: (i, 0)), pl.BlockSpec((k, bn), lambda i, j: (0, j))],
        out_specs=pl.BlockSpec((bm, bn), lambda i, j: (i, j)),
        compiler_params=_cparams(("arbitrary", "arbitrary")),
        name="matmul",
    )(a, b)


def _mm_res_kernel(a_ref, b_ref, x_ref, g_ref, o_ref):
    o_ref[...] = x_ref[...] + g_ref[0] * _dot(a_ref[...], b_ref[...])


def matmul_gated_residual(a, b, x, mod3, gate_chunk, mod_row, *, bn):
    m, k = a.shape
    n = b.shape[1]
    nb = n // bn
    return pl.pallas_call(
        _mm_res_kernel,
        out_shape=jax.ShapeDtypeStruct((m, n), F32),
        grid=(nb, m // ROW_TILE),
        in_specs=[
            pl.BlockSpec((ROW_TILE, k), lambda j, i: (i, 0)),
            pl.BlockSpec((k, bn), lambda j, i: (0, j)),
            pl.BlockSpec((ROW_TILE, bn), lambda j, i: (i, j)),
            pl.BlockSpec((1, 1, bn), lambda j, i: (mod_row(i), 0, gate_chunk * nb + j)),
        ],
        out_specs=pl.BlockSpec((ROW_TILE, bn), lambda j, i: (i, j)),
        compiler_params=_cparams(("arbitrary", "arbitrary")),
        name="matmul_gated_residual",
    )(a, b, x, mod3)


def _gla_kernel(q_ref, k_ref, v_ref, paa_ref, wa_ref, ba_ref, o_ref, st_ref, *, nh):
    c = GLA_CHUNK
    d = pl.program_id(1)

    @pl.when(pl.program_id(2) == 0)
    def _():
        st_ref[...] = jnp.zeros_like(st_ref)

    row = lax.broadcasted_iota(jnp.int32, (c, c), 0)
    col = lax.broadcasted_iota(jnp.int32, (c, c), 1)
    incl = jnp.where(d == 0, col - row, row - col) <= 0
    tri = incl.astype(BF16)

    z = _dot(paa_ref[...].astype(BF16), wa_ref[0].astype(BF16)) + ba_ref[0]
    la = _log_sigmoid(z) * (1.0 / GLA_TAU)
    la_hi = la.astype(BF16)
    la_lo = (la - la_hi.astype(F32)).astype(BF16)
    b = _dot(tri, la_hi) + _dot(tri, la_lo)
    tot = jnp.sum(la, axis=0, keepdims=True)
    mid = 0.5 * tot
    e_q = jnp.exp(b - mid)
    e_k = jnp.exp(mid - b)
    e_in = jnp.exp(b)
    e_out = jnp.exp(tot - b)
    e_tot = jnp.exp(tot)

    for h in range(nh):
        ks = slice(h * GLA_DK, (h + 1) * GLA_DK)
        vs = slice(h * GLA_DV, (h + 1) * GLA_DV)
        q = q_ref[:, ks].astype(F32) * (GLA_DK ** -0.5)
        k = k_ref[:, ks].astype(F32)
        v = v_ref[:, vs]
        s = _dot_nt((q * e_q[:, ks]).astype(BF16), (k * e_k[:, ks]).astype(BF16))
        s = jnp.where(incl, s, 0.0)
        st = st_ref[h]
        o = _dot(s.astype(BF16), v) + _dot_nt((q * e_in[:, ks]).astype(BF16), st.astype(BF16))
        st_ref[h] = st * e_tot[:, ks] + _dot_tn(v, (k * e_out[:, ks]).astype(BF16))
        o_ref[0, :, vs] = o


def gla_scan(p, paa, wa_blk, ba_blk, *, nb, rows_per_batch, n_ctx_chunks, nh):
    t = p.shape[0]
    c = GLA_CHUNK
    ncb = rows_per_batch // c
    kw = nh * GLA_DK
    vw = nh * GLA_DV

    def chunk(b, d, g):
        bwd = jnp.where(g < n_ctx_chunks, n_ctx_chunks - 1 - g, ncb - 1 + n_ctx_chunks - g)
        return b * ncb + jnp.where(d == 0, g, bwd)

    return pl.pallas_call(
        functools.partial(_gla_kernel, nh=nh),
        out_shape=jax.ShapeDtypeStruct((2, t, vw), F32),
        grid=(nb, 2, ncb),
        in_specs=[
            pl.BlockSpec((c, kw), lambda b, d, g: (chunk(b, d, g), 0)),
            pl.BlockSpec((c, kw), lambda b, d, g: (chunk(b, d, g), 1)),
            pl.BlockSpec((c, vw), lambda b, d, g: (chunk(b, d, g), 2 * kw // vw)),
            pl.BlockSpec((c, LANES), lambda b, d, g: (chunk(b, d, g), 0)),
            pl.BlockSpec((1, LANES, kw), lambda b, d, g: (d, 0, 0)),
            pl.BlockSpec((1, 1, kw), lambda b, d, g: (d, 0, 0)),
        ],
        out_specs=pl.BlockSpec((1, c, vw), lambda b, d, g: (d, chunk(b, d, g), 0)),
        scratch_shapes=[pltpu.VMEM((nh, GLA_DV, GLA_DK), F32)],
        compiler_params=_cparams(("arbitrary", "arbitrary", "arbitrary")),
        name="gla_scan",
    )(p, p, p, paa, wa_blk, ba_blk)


def _gla_out_kernel(o_ref, gate_ref, gn_ref, out_ref, *, nh):
    o = o_ref[0] + o_ref[1]
    for h in range(nh):
        vs = slice(h * GLA_DV, (h + 1) * GLA_DV)
        gate = gate_ref[:, vs].astype(F32)
        out_ref[:, vs] = (_rms(o[:, vs], gn_ref[...]) * _silu(gate)).astype(out_ref.dtype)


def gla_output(o2, p, gn, *, gate_block, nh):
    t = p.shape[0]
    vw = nh * GLA_DV
    return pl.pallas_call(
        functools.partial(_gla_out_kernel, nh=nh),
        out_shape=jax.ShapeDtypeStruct((t, vw), BF16),
        grid=(t // ROW_TILE,),
        in_specs=[
            pl.BlockSpec((2, ROW_TILE, vw), lambda i: (0, i, 0)),
            pl.BlockSpec((ROW_TILE, vw), lambda i: (i, gate_block)),
            pl.BlockSpec((1, GLA_DV), lambda i: (0, 0)),
        ],
        out_specs=pl.BlockSpec((ROW_TILE, vw), lambda i: (i, 0)),
        compiler_params=_cparams(("arbitrary",)),
        name="gla_output",
    )(o2, p, gn.reshape(1, GLA_DV))


def _rope(x, cos, sin_signed):
    lane = lax.broadcasted_iota(jnp.int32, x.shape, 1)
    first = (lane % (HEAD_DIM // 2)) < (HEAD_DIM // 4)
    rot = jnp.where(first, pltpu.roll(x, HEAD_DIM - HEAD_DIM // 4, 1), pltpu.roll(x, HEAD_DIM // 4, 1))
    return x * cos + rot * sin_signed


def rope_tables(n_ctx, seq):
    nf = HEAD_DIM // 4
    rows = seq // GRID_W
    row = jnp.repeat(jnp.arange(rows, dtype=jnp.int32), GRID_W).astype(F32)
    col = jnp.tile(jnp.arange(GRID_W, dtype=jnp.int32), rows).astype(F32)
    inv = ROPE_THETA ** (-jnp.arange(nf, dtype=F32) / nf)
    ang = jnp.concatenate([row[:, None] * inv, row[:, None] * inv, col[:, None] * inv, col[:, None] * inv], axis=1)
    sign = jnp.tile(jnp.concatenate([-jnp.ones((nf,), F32), jnp.ones((nf,), F32)]), 2)
    cos = jnp.concatenate([jnp.ones((n_ctx, HEAD_DIM), F32), jnp.cos(ang)], axis=0)
    sin = jnp.concatenate([jnp.zeros((n_ctx, HEAD_DIM), F32), jnp.sin(ang) * sign], axis=0)
    return cos, sin


def _softmax_parts(s):
    m = jnp.max(s, axis=-1, keepdims=True)
    e = jnp.exp(s - m)
    return e, jnp.sum(e, axis=-1, keepdims=True)


def _gqa_kernel(q_ref, k_ref, v_ref, cq_ref, sq_ref, ck_ref, sk_ref, qn_ref, kn_ref, o_ref, ks_ref, *,
                n_ctx, ctx_tile_first):
    qi = pl.program_id(2)

    @pl.when(qi == 0)
    def _():
        k = _rms(k_ref[...].astype(F32), kn_ref[...])
        ks_ref[...] = _rope(k, ck_ref[...], sk_ref[...]).astype(BF16)

    def attend(nk):
        keys = ks_ref[0:nk, :]
        vals = v_ref[0:nk, :]
        for g in range(GQA_GROUP):
            hs = slice(g * HEAD_DIM, (g + 1) * HEAD_DIM)
            q = _rms(q_ref[:, hs].astype(F32), qn_ref[...])
            q = (_rope(q, cq_ref[...], sq_ref[...]) * (HEAD_DIM ** -0.5)).astype(BF16)
            e, l = _softmax_parts(_dot_nt(q, keys))
            o_ref[:, hs] = (_dot(e.astype(BF16), vals) / l).astype(o_ref.dtype)

    if ctx_tile_first:
        @pl.when(qi == 0)
        def _():
            attend(n_ctx)

        @pl.when(qi > 0)
        def _():
            attend(k_ref.shape[0])
    else:
        attend(k_ref.shape[0])


def gqa_attention(p, cos, sin, qn, kn, *, nb, rows_per_batch, n_ctx, n_kv, q_col, k_col, v_col, with_ctx):
    t = p.shape[0]
    tpb = rows_per_batch // ROW_TILE
    nq = tpb if with_ctx else tpb - 1
    off = 0 if with_ctx else 1
    qw = GQA_GROUP * HEAD_DIM
    return pl.pallas_call(
        functools.partial(_gqa_kernel, n_ctx=n_ctx, ctx_tile_first=with_ctx),
        out_shape=jax.ShapeDtypeStruct((t, n_kv * qw), BF16),
        grid=(nb, n_kv, nq),
        in_specs=[
            pl.BlockSpec((ROW_TILE, qw), lambda b, h, i: (b * tpb + i + off, q_col // qw + h)),
            pl.BlockSpec((rows_per_batch, HEAD_DIM), lambda b, h, i: (b, k_col // HEAD_DIM + h)),
            pl.BlockSpec((rows_per_batch, HEAD_DIM), lambda b, h, i: (b, v_col // HEAD_DIM + h)),
            pl.BlockSpec((ROW_TILE, HEAD_DIM), lambda b, h, i: (i + off, 0)),
            pl.BlockSpec((ROW_TILE, HEAD_DIM), lambda b, h, i: (i + off, 0)),
            pl.BlockSpec((rows_per_batch, HEAD_DIM), lambda b, h, i: (0, 0)),
            pl.BlockSpec((rows_per_batch, HEAD_DIM), lambda b, h, i: (0, 0)),
            pl.BlockSpec((1, HEAD_DIM), lambda b, h, i: (0, 0)),
            pl.BlockSpec((1, HEAD_DIM), lambda b, h, i: (0, 0)),
        ],
        out_specs=pl.BlockSpec((ROW_TILE, qw), lambda b, h, i: (b * tpb + i + off, h)),
        scratch_shapes=[pltpu.VMEM((rows_per_batch, HEAD_DIM), BF16)],
        compiler_params=_cparams(("arbitrary", "arbitrary", "arbitrary")),
        name="gqa_attention",
    )(p, p, p, cos, sin, cos, sin, qn.reshape(1, HEAD_DIM), kn.reshape(1, HEAD_DIM))


def _diff_kernel(q_ref, k_ref, v_ref, cq_ref, sq_ref, ck_ref, sk_ref, lq1_ref, lk1_ref, lq2_ref, lk2_ref, gn_ref,
                 o_ref, ks_ref, *, n_ctx, ctx_tile_first, lam_init):
    qi = pl.program_id(2)

    @pl.when(qi == 0)
    def _():
        for j in range(2):
            hs = slice(j * HEAD_DIM, (j + 1) * HEAD_DIM)
            ks_ref[:, hs] = _rope(k_ref[:, hs].astype(F32), ck_ref[...], sk_ref[...]).astype(BF16)

    lam = (jnp.exp(jnp.sum(lq1_ref[...] * lk1_ref[...], axis=-1, keepdims=True))
           - jnp.exp(jnp.sum(lq2_ref[...] * lk2_ref[...], axis=-1, keepdims=True)) + lam_init)

    def attend(nk):
        w = None
        for j in range(2):
            hs = slice(j * HEAD_DIM, (j + 1) * HEAD_DIM)
            q = (_rope(q_ref[:, hs].astype(F32), cq_ref[...], sq_ref[...]) * (HEAD_DIM ** -0.5)).astype(BF16)
            e, l = _softmax_parts(_dot_nt(q, ks_ref[0:nk, hs]))
            pj = e / l
            w = pj if j == 0 else w - lam * pj
        o = _dot(w.astype(BF16), v_ref[0:nk, :])
        o_ref[...] = (_rms(o, gn_ref[...]) * (1.0 - lam_init)).astype(o_ref.dtype)

    if ctx_tile_first:
        @pl.when(qi == 0)
        def _():
            attend(n_ctx)

        @pl.when(qi > 0)
        def _():
            attend(k_ref.shape[0])
    else:
        attend(k_ref.shape[0])


def diff_attention(p, cos, sin, lq1, lk1, lq2, lk2, gn, *, nb, rows_per_batch, n_ctx, nh, q_col, k_col, v_col,
                   with_ctx, lam_init):
    t = p.shape[0]
    tpb = rows_per_batch // ROW_TILE
    nq = tpb if with_ctx else tpb - 1
    off = 0 if with_ctx else 1
    w2 = 2 * HEAD_DIM
    vec = pl.BlockSpec((1, HEAD_DIM), lambda b, h, i: (0, 0))
    return pl.pallas_call(
        functools.partial(_diff_kernel, n_ctx=n_ctx, ctx_tile_first=with_ctx, lam_init=lam_init),
        out_shape=jax.ShapeDtypeStruct((t, nh * DIFF_DV), BF16),
        grid=(nb, nh, nq),
        in_specs=[
            pl.BlockSpec((ROW_TILE, w2), lambda b, h, i: (b * tpb + i + off, q_col // w2 + h)),
            pl.BlockSpec((rows_per_batch, w2), lambda b, h, i: (b, k_col // w2 + h)),
            pl.BlockSpec((rows_per_batch, DIFF_DV), lambda b, h, i: (b, v_col // DIFF_DV + h)),
            pl.BlockSpec((ROW_TILE, HEAD_DIM), lambda b, h, i: (i + off, 0)),
            pl.BlockSpec((ROW_TILE, HEAD_DIM), lambda b, h, i: (i + off, 0)),
            pl.BlockSpec((rows_per_batch, HEAD_DIM), lambda b, h, i: (0, 0)),
            pl.BlockSpec((rows_per_batch, HEAD_DIM), lambda b, h, i: (0, 0)),
            vec, vec, vec, vec,
            pl.BlockSpec((1, DIFF_DV), lambda b, h, i: (0, 0)),
        ],
        out_specs=pl.BlockSpec((ROW_TILE, DIFF_DV), lambda b, h, i: (b * tpb + i + off, h)),
        scratch_shapes=[pltpu.VMEM((rows_per_batch, w2), BF16)],
        compiler_params=_cparams(("arbitrary", "arbitrary", "arbitrary")),
        name="diff_attention",
    )(p, p, p, cos, sin, cos, sin, lq1.reshape(1, -1), lk1.reshape(1, -1), lq2.reshape(1, -1), lk2.reshape(1, -1),
      gn.reshape(1, DIFF_DV))


def _route_kernel(lg_ref, bias_ref, o_ref):
    x = lg_ref[...] + bias_ref[...]
    lane = lax.broadcasted_iota(jnp.int32, x.shape, 1).astype(F32)
    neg = -jnp.inf

    def first_max(vals, mask):
        v = jnp.where(mask, vals, neg)
        m = jnp.max(v, axis=-1, keepdims=True)
        idx = jnp.min(jnp.where(mask & (v == m), lane, float(LANES)), axis=-1, keepdims=True)
        return m, idx

    gmask = lane < N_GROUPS
    gm, gidx = first_max(x, gmask)
    g_w = 1.0 / jnp.sum(jnp.where(gmask, jnp.exp(x - gm), 0.0), axis=-1, keepdims=True)
    lo = N_GROUPS + EXPERTS_PER_GROUP * gidx
    emask = (lane >= lo) & (lane < lo + EXPERTS_PER_GROUP)
    m1, i1 = first_max(x, emask)
    m2, i2 = first_max(x, emask & (lane != i1))
    r = jnp.exp(m2 - m1)
    w1 = g_w / (1.0 + r)
    w2 = g_w * r / (1.0 + r)
    out = jnp.where(lane == 0, i1 - N_GROUPS, 0.0)
    out = jnp.where(lane == 1, i2 - N_GROUPS, out)
    out = jnp.where(lane == 2, w1, out)
    out = jnp.where(lane == 3, w2, out)
    o_ref[...] = out


def route(logits, bias):
    t = logits.shape[0]
    return pl.pallas_call(
        _route_kernel,
        out_shape=jax.ShapeDtypeStruct((t, LANES), F32),
        grid=(t // ROW_TILE,),
        in_specs=[pl.BlockSpec((ROW_TILE, LANES), lambda i: (i, 0)), pl.BlockSpec((1, LANES), lambda i: (0, 0))],
        out_specs=pl.BlockSpec((ROW_TILE, LANES), lambda i: (i, 0)),
        compiler_params=_cparams(("arbitrary",)),
        name="route",
    )(logits, bias)


def dispatch_plan(ids, wts, n_tiles):
    t = ids.shape[0]
    e = ids.reshape(-1)
    order = jnp.argsort(e, stable=True).astype(jnp.int32)
    e_sorted = e[order]
    counts = jnp.zeros((N_EXPERTS,), jnp.int32).at[e].add(1)
    tiles_per = (counts + MOE_TILE - 1) // MOE_TILE
    tile_end = jnp.cumsum(tiles_per)
    row_start = (tile_end - tiles_per) * MOE_TILE
    grp_start = jnp.cumsum(counts) - counts
    pos_sorted = row_start[e_sorted] + jnp.arange(2 * t, dtype=jnp.int32) - grp_start[e_sorted]
    n_rows = n_tiles * MOE_TILE
    src = jnp.zeros((n_rows,), jnp.int32).at[pos_sorted].set(order // 2)
    wrow = jnp.zeros((n_rows,), F32).at[pos_sorted].set(wts.reshape(-1)[order])
    pos = jnp.zeros((2 * t,), jnp.int32).at[order].set(pos_sorted).reshape(t, 2)
    tile_expert = jnp.minimum(jnp.searchsorted(tile_end, jnp.arange(n_tiles, dtype=jnp.int32), side="right"),
                              N_EXPERTS - 1).astype(jnp.int32)
    n_used = tile_end[-1:].astype(jnp.int32)
    return src, wrow, pos, tile_expert, n_used


def _row_copy(src_hbm, idx, dst_vmem, r, sem):
    return pltpu.make_async_copy(src_hbm.at[pl.ds(idx, 1)], dst_vmem.at[pl.ds(r, 1)], sem)


def _gather_kernel(src_ref, nused_ref, h_hbm, o_ref, sem):
    j = pl.program_id(0)
    rows = o_ref.shape[0]

    @pl.when(j < nused_ref[0])
    def _():
        def start(r, carry):
            _row_copy(h_hbm, src_ref[j * rows + r], o_ref, r, sem).start()
            return carry

        lax.fori_loop(0, rows, start, 0)
        pltpu.make_async_copy(h_hbm.at[pl.ds(0, rows)], o_ref, sem).wait()

    @pl.when(j >= nused_ref[0])
    def _():
        o_ref[...] = jnp.zeros_like(o_ref)


def gather_rows(h, src, n_used, n_tiles):
    d = h.shape[1]
    return pl.pallas_call(
        _gather_kernel,
        out_shape=jax.ShapeDtypeStruct((n_tiles * MOE_TILE, d), h.dtype),
        grid_spec=pltpu.PrefetchScalarGridSpec(
            num_scalar_prefetch=2,
            grid=(n_tiles,),
            in_specs=[pl.BlockSpec(memory_space=pl.ANY)],
            out_specs=pl.BlockSpec((MOE_TILE, d), lambda j, s, n: (j, 0)),
            scratch_shapes=[pltpu.SemaphoreType.DMA],
        ),
        compiler_params=_cparams(("arbitrary",)),
        name="moe_gather",
    )(src, n_used, h)


def _expert_kernel(te_ref, nused_ref, x_ref, wg_ref, wu_ref, wd_ref, wr_ref, y_ref):
    j = pl.program_id(0)

    @pl.when(j < nused_ref[0])
    def _():
        x = x_ref[...].astype(BF16)
        a = _dot(x, wg_ref[0])
        u = _dot(x, wu_ref[0])
        hid = (_silu(a) * u * wr_ref[...]).astype(BF16)
        y_ref[...] = _dot(hid, wd_ref[0])

    @pl.when(j >= nused_ref[0])
    def _():
        y_ref[...] = jnp.zeros_like(y_ref)


def expert_mlp(xg, wrow, wg, wu, wd, tile_expert, n_used):
    n_rows, d = xg.shape
    f = wg.shape[2]
    n_tiles = n_rows // MOE_TILE
    return pl.pallas_call(
        _expert_kernel,
        out_shape=jax.ShapeDtypeStruct((n_rows, d), F32),
        grid_spec=pltpu.PrefetchScalarGridSpec(
            num_scalar_prefetch=2,
            grid=(n_tiles,),
            in_specs=[
                pl.BlockSpec((MOE_TILE, d), lambda j, te, n: (j, 0)),
                pl.BlockSpec((1, d, f), lambda j, te, n: (te[j], 0, 0)),
                pl.BlockSpec((1, d, f), lambda j, te, n: (te[j], 0, 0)),
                pl.BlockSpec((1, f, d), lambda j, te, n: (te[j], 0, 0)),
                pl.BlockSpec((MOE_TILE, 1), lambda j, te, n: (j, 0)),
            ],
            out_specs=pl.BlockSpec((MOE_TILE, d), lambda j, te, n: (j, 0)),
        ),
        compiler_params=_cparams(("arbitrary",)),
        name="moe_experts",
    )(tile_expert, n_used, xg, wg, wu, wd, wrow.reshape(n_rows, 1))


def _combine_kernel(p1_ref, p2_ref, x_ref, g_ref, y_hbm, o_ref, buf_ref, sem):
    i = pl.program_id(0)
    rows = x_ref.shape[0]

    def start(r, carry):
        _row_copy(y_hbm, p1_ref[i * rows + r], buf_ref.at[0], r, sem).start()
        _row_copy(y_hbm, p2_ref[i * rows + r], buf_ref.at[1], r, sem).start()
        return carry

    lax.fori_loop(0, rows, start, 0)
    pltpu.make_async_copy(y_hbm.at[pl.ds(0, rows)], buf_ref.at[0], sem).wait()
    pltpu.make_async_copy(y_hbm.at[pl.ds(0, rows)], buf_ref.at[1], sem).wait()
    o_ref[...] = x_ref[...] + g_ref[0] * (buf_ref[0] + buf_ref[1])


def moe_combine(x, y, pos, mod3, gate_chunk, mod_row):
    t, d = x.shape
    return pl.pallas_call(
        _combine_kernel,
        out_shape=jax.ShapeDtypeStruct((t, d), F32),
        grid_spec=pltpu.PrefetchScalarGridSpec(
            num_scalar_prefetch=2,
            grid=(t // ROW_TILE,),
            in_specs=[
                pl.BlockSpec((ROW_TILE, d), lambda i, a, b: (i, 0)),
                pl.BlockSpec((1, 1, d), lambda i, a, b: (mod_row(i), 0, gate_chunk)),
                pl.BlockSpec(memory_space=pl.ANY),
            ],
            out_specs=pl.BlockSpec((ROW_TILE, d), lambda i, a, b: (i, 0)),
            scratch_shapes=[pltpu.VMEM((2, ROW_TILE, d), F32), pltpu.SemaphoreType.DMA],
        ),
        compiler_params=_cparams(("arbitrary",)),
        name="moe_combine",
    )(pos[:, 0], pos[:, 1], x, mod3, y)


class _LatentTiles:
    def __init__(self, nb, tiles_per_batch):
        self.n_out = nb * (tiles_per_batch - 1)
        self._lat = tiles_per_batch - 1
        self._tpb = tiles_per_batch

    def __call__(self, i):
        return (i // self._lat) * self._tpb + 1 + i % self._lat


def kernel(x, c, ctx, c_ctx, ada_w, ada_b, norm1_g, w_in, gla_wa_fwd, gla_ba_fwd, gla_wa_bwd, gla_ba_bwd, gla_norm_g,
           gqa_qnorm_g, gqa_knorm_g, diff_lq1, diff_lk1, diff_lq2, diff_lk2, diff_norm_g, w_out, norm2_g, router_wg,
           router_bg, router_we, router_be, moe_w_gate, moe_w_up, moe_w_down, final_norm_g):
    nb, seq, d = x.shape
    n_ctx = ctx.shape[1]
    depth = ada_w.shape[0]
    assert n_ctx == ROW_TILE and seq % ROW_TILE == 0 and seq % GRID_W == 0
    rpb = n_ctx + seq
    tpb = rpb // ROW_TILE
    t = nb * rpb

    gla_heads = (d // 4) // GLA_DV
    gqa_heads = (d // 2) // HEAD_DIM
    n_kv = gqa_heads // GQA_GROUP
    diff_heads = (d // 4) // DIFF_DV
    kw = gla_heads * GLA_DK
    vw = gla_heads * GLA_DV
    splits = (kw, kw, vw, vw, GLA_RANK, GLA_RANK, gqa_heads * HEAD_DIM, n_kv * HEAD_DIM, n_kv * HEAD_DIM,
              diff_heads * 2 * HEAD_DIM, diff_heads * 2 * HEAD_DIM, diff_heads * DIFF_DV)
    offs = [0]
    for s in splits:
        offs.append(offs[-1] + s)
    wide_cols = jnp.concatenate([jnp.arange(offs[0], offs[4]), jnp.arange(offs[6], offs[12])])
    col = {}
    acc = 0
    for name, width in (("aq", kw), ("ak", kw), ("av", vw), ("ag", vw), ("bq", splits[6]), ("bk", splits[7]),
                        ("bv", splits[8]), ("dq", splits[9]), ("dk", splits[10]), ("dv", splits[11])):
        col[name] = acc
        acc += width
    wide = acc

    def mod_row(i):
        return jnp.where(i % tpb == 0, nb, i // tpb)

    tokens = jnp.concatenate([ctx, x], axis=1).reshape(t, d)
    c_all = jnp.zeros((8, d), F32).at[:nb].set(c).at[nb].set(c_ctx)
    mod = ada_modulation(c_all, ada_w, ada_b)
    cos, sin = rope_tables(n_ctx, seq)
    n_tiles = (2 * t) // MOE_TILE + N_EXPERTS

    for l in range(depth):
        last = l == depth - 1
        lam_init = 0.8 - 0.6 * math.exp(-0.3 * l)
        mod3 = mod[l].reshape(8, 1, 6 * d)

        w_wide = w_in[l][:, wide_cols].astype(BF16)
        w_dec = jnp.zeros((d, LANES), F32).at[:, :2 * GLA_RANK].set(w_in[l][:, offs[4]:offs[6]])
        h, paa = norm_modulate(tokens, norm1_g[l], mod3, 1, 0, w_dec, mod_row, h_dtype=BF16, exact_small=False)
        p = matmul(h, w_wide, bm=1024, bn=1024, out_dtype=BF16)

        wa_blk = jnp.zeros((2, LANES, kw), F32)
        wa_blk = wa_blk.at[0, :GLA_RANK].set(gla_wa_fwd[l]).at[1, GLA_RANK:2 * GLA_RANK].set(gla_wa_bwd[l])
        ba_blk = jnp.stack([gla_ba_fwd[l], gla_ba_bwd[l]]).reshape(2, 1, kw)
        o2 = gla_scan(p, paa, wa_blk, ba_blk, nb=nb, rows_per_batch=rpb, n_ctx_chunks=n_ctx // GLA_CHUNK,
                      nh=gla_heads)
        mix_a = gla_output(o2, p, gla_norm_g[l], gate_block=col["ag"] // vw, nh=gla_heads)
        mix_b = gqa_attention(p, cos, sin, gqa_qnorm_g[l], gqa_knorm_g[l], nb=nb, rows_per_batch=rpb, n_ctx=n_ctx,
                              n_kv=n_kv, q_col=col["bq"], k_col=col["bk"], v_col=col["bv"], with_ctx=True)
        mix_c = diff_attention(p, cos, sin, diff_lq1[l], diff_lk1[l], diff_lq2[l], diff_lk2[l], diff_norm_g[l],
                               nb=nb, rows_per_batch=rpb, n_ctx=n_ctx, nh=diff_heads, q_col=col["dq"],
                               k_col=col["dk"], v_col=col["dv"], with_ctx=True, lam_init=lam_init)
        mix = jnp.concatenate([mix_a, mix_b, mix_c], axis=1)
        tokens = matmul_gated_residual(mix, w_out[l].astype(BF16), tokens, mod3, 2, mod_row, bn=1024)

        w_route = jnp.zeros((d, LANES), F32).at[:, :N_GROUPS].set(router_wg[l])
        w_route = w_route.at[:, N_GROUPS:N_GROUPS + N_EXPERTS].set(router_we[l])
        b_route = jnp.zeros((1, LANES), F32).at[0, :N_GROUPS].set(router_bg[l])
        b_route = b_route.at[0, N_GROUPS:N_GROUPS + N_EXPERTS].set(router_be[l])
        h2, logits = norm_modulate(tokens, norm2_g[l], mod3, 4, 3, w_route, mod_row, h_dtype=F32, exact_small=True)
        routed = route(logits, b_route)
        ids = routed[:, 0:2].astype(jnp.int32)
        wts = routed[:, 2:4]
        src, wrow, pos, tile_expert, n_used = dispatch_plan(ids, wts, n_tiles)
        xg = gather_rows(h2, src, n_used, n_tiles)
        f = moe_w_gate.shape[-1]
        y = expert_mlp(xg, wrow, moe_w_gate[l].reshape(N_EXPERTS, d, f).astype(BF16),
                       moe_w_up[l].reshape(N_EXPERTS, d, f).astype(BF16),
                       moe_w_down[l].reshape(N_EXPERTS, f, d).astype(BF16), tile_expert, n_used)
        tokens = moe_combine(tokens, y, pos, mod3, 5, mod_row)

    out = final_norm(tokens, final_norm_g, _LatentTiles(nb, tpb))
    return out.reshape(nb, seq, d)
```

```python
import functools
import math

import jax
import jax.numpy as jnp
from jax import lax
from jax.experimental import pallas as pl
from jax.experimental.pallas import tpu as pltpu

F32 = jnp.float32
BF16 = jnp.bfloat16

HEAD_DIM = 128
GRID_W = 64
ROPE_THETA = 10000.0
EPS = 1e-6
GLA_DK = HEAD_DIM
GLA_DV = 2 * HEAD_DIM
GLA_RANK = 16
GLA_TAU = 16.0
GLA_CHUNK = 64
GQA_GROUP = 4
DIFF_DV = 2 * HEAD_DIM
N_GROUPS = 4
EXPERTS_PER_GROUP = 8
N_EXPERTS = N_GROUPS * EXPERTS_PER_GROUP

LANES = 128
ROW_TILE = 256
MOE_TILE = 256
W_CHUNKS = 8
VMEM_LIMIT = 56 * 1024 * 1024


def _cparams(sem, vmem=VMEM_LIMIT):
    return pltpu.CompilerParams(dimension_semantics=sem, vmem_limit_bytes=vmem)


def _sigmoid(x):
    return 1.0 / (1.0 + jnp.exp(-x))


def _silu(x):
    return x * _sigmoid(x)


def _log_sigmoid(x):
    return jnp.minimum(x, 0.0) - jnp.log(1.0 + jnp.exp(-jnp.abs(x)))


def _rms(x, g):
    return x * lax.rsqrt(jnp.mean(x * x, axis=-1, keepdims=True) + EPS) * g


def _dot(a, b):
    return jnp.dot(a, b, preferred_element_type=F32)


def _dot_nt(a, b):
    return lax.dot_general(a, b, (((1,), (1,)), ((), ())), preferred_element_type=F32)


def _dot_tn(a, b):
    return lax.dot_general(a, b, (((0,), (0,)), ((), ())), preferred_element_type=F32)


def _ada_kernel(c_ref, w_ref, b_ref, o_ref):
    s = _silu(c_ref[...]).astype(BF16)
    o_ref[0] = _dot(s, w_ref[0].astype(BF16)) + b_ref[0]


def ada_modulation(c_all, ada_w, ada_b, *, tn=1024):
    nl, d, n = ada_w.shape
    return pl.pallas_call(
        _ada_kernel,
        out_shape=jax.ShapeDtypeStruct((nl, 8, n), F32),
        grid=(nl, n // tn),
        in_specs=[
            pl.BlockSpec((8, d), lambda l, j: (0, 0)),
            pl.BlockSpec((1, d, tn), lambda l, j: (l, 0, j)),
            pl.BlockSpec((1, 1, tn), lambda l, j: (l, 0, j)),
        ],
        out_specs=pl.BlockSpec((1, 8, tn), lambda l, j: (l, 0, j)),
        compiler_params=_cparams(("arbitrary", "arbitrary")),
        name="ada_modulation",
    )(c_all, ada_w, ada_b.reshape(nl, 1, n))


def _norm_kernel(x_ref, g_ref, sc_ref, sh_ref, w_ref, h_ref, s_ref, *, exact_small):
    h = _rms(x_ref[...], g_ref[...]) * (1.0 + sc_ref[0]) + sh_ref[0]
    h_ref[...] = h.astype(h_ref.dtype)
    if exact_small:
        s_ref[...] = jnp.dot(h, w_ref[...], precision=lax.Precision.HIGHEST, preferred_element_type=F32)
    else:
        s_ref[...] = _dot(h.astype(BF16), w_ref[...].astype(BF16))


def norm_modulate(x, g, mod3, sc_chunk, sh_chunk, w_small, mod_row, *, h_dtype, exact_small):
    t, d = x.shape
    ns = w_small.shape[1]
    return pl.pallas_call(
        functools.partial(_norm_kernel, exact_small=exact_small),
        out_shape=(jax.ShapeDtypeStruct((t, d), h_dtype), jax.ShapeDtypeStruct((t, ns), F32)),
        grid=(t // ROW_TILE,),
        in_specs=[
            pl.BlockSpec((ROW_TILE, d), lambda i: (i, 0)),
            pl.BlockSpec((1, d), lambda i: (0, 0)),
            pl.BlockSpec((1, 1, d), lambda i: (mod_row(i), 0, sc_chunk)),
            pl.BlockSpec((1, 1, d), lambda i: (mod_row(i), 0, sh_chunk)),
            pl.BlockSpec((d, ns), lambda i: (0, 0)),
        ],
        out_specs=(pl.BlockSpec((ROW_TILE, d), lambda i: (i, 0)), pl.BlockSpec((ROW_TILE, ns), lambda i: (i, 0))),
        compiler_params=_cparams(("arbitrary",)),
        name="norm_modulate",
    )(x, g.reshape(1, d), mod3, mod3, w_small)


def _final_norm_kernel(x_ref, g_ref, o_ref):
    o_ref[...] = _rms(x_ref[...], g_ref[...])


def final_norm(x, g, row_block):
    d = x.shape[1]
    n_out = row_block.n_out
    return pl.pallas_call(
        _final_norm_kernel,
        out_shape=jax.ShapeDtypeStruct((n_out * ROW_TILE, d), F32),
        grid=(n_out,),
        in_specs=[pl.BlockSpec((ROW_TILE, d), lambda i: (row_block(i), 0)), pl.BlockSpec((1, d), lambda i: (0, 0))],
        out_specs=pl.BlockSpec((ROW_TILE, d), lambda i: (i, 0)),
        compiler_params=_cparams(("arbitrary",)),
        name="final_norm",
    )(x, g.reshape(1, d))


def _mm_kernel(a_ref, b_ref, o_ref):
    o_ref[...] = _dot(a_ref[...], b_ref[...]).astype(o_ref.dtype)


def matmul(a, b, *, bm, bn, out_dtype):
    m, k = a.shape
    n = b.shape[1]
    return pl.pallas_call(
        _mm_kernel,
        out_shape=jax.ShapeDtypeStruct((m, n), out_dtype),
        grid=(m // bm, n // bn),
        in_specs=[pl.BlockSpec((bm, k), lambda i, j: (i, 0)), pl.BlockSpec((k, bn), lambda i, j: (0, j))],
        out_specs=pl.BlockSpec((bm, bn), lambda i, j: (i, j)),
        compiler_params=_cparams(("arbitrary", "arbitrary")),
        name="matmul",
    )(a, b)


def _mm_res_kernel(a_ref, b_ref, x_ref, g_ref, o_ref):
    o_ref[...] = x_ref[...] + g_ref[0] * _dot(a_ref[...], b_ref[...])


def matmul_gated_residual(a, b, x, mod3, gate_chunk, mod_row, *, bn):
    m, k = a.shape
    n = b.shape[1]
    nb = n // bn
    return pl.pallas_call(
        _mm_res_kernel,
        out_shape=jax.ShapeDtypeStruct((m, n), F32),
        grid=(nb, m // ROW_TILE),
        in_specs=[
            pl.BlockSpec((ROW_TILE, k), lambda j, i: (i, 0)),
            pl.BlockSpec((k, bn), lambda j, i: (0, j)),
            pl.BlockSpec((ROW_TILE, bn), lambda j, i: (i, j)),
            pl.BlockSpec((1, 1, bn), lambda j, i: (mod_row(i), 0, gate_chunk * nb + j)),
        ],
        out_specs=pl.BlockSpec((ROW_TILE, bn), lambda j, i: (i, j)),
        compiler_params=_cparams(("arbitrary", "arbitrary")),
        name="matmul_gated_residual",
    )(a, b, x, mod3)


def _gla_kernel(q_ref, k_ref, v_ref, paa_ref, wa_ref, ba_ref, o_ref, st_ref, *, nh):
    c = GLA_CHUNK
    d = pl.program_id(1)

    @pl.when(pl.program_id(2) == 0)
    def _():
        st_ref[...] = jnp.zeros_like(st_ref)

    row = lax.broadcasted_iota(jnp.int32, (c, c), 0)
    col = lax.broadcasted_iota(jnp.int32, (c, c), 1)
    incl = jnp.where(d == 0, col - row, row - col) <= 0
    tri = incl.astype(BF16)

    z = _dot(paa_ref[...].astype(BF16), wa_ref[0].astype(BF16)) + ba_ref[0]
    la = _log_sigmoid(z) * (1.0 / GLA_TAU)
    la_hi = la.astype(BF16)
    la_lo = (la - la_hi.astype(F32)).astype(BF16)
    b = _dot(tri, la_hi) + _dot(tri, la_lo)
    tot = jnp.sum(la, axis=0, keepdims=True)
    mid = 0.5 * tot
    e_q = jnp.exp(b - mid)
    e_k = jnp.exp(mid - b)
    e_in = jnp.exp(b)
    e_out = jnp.exp(tot - b)
    e_tot = jnp.exp(tot)

    for h in range(nh):
        ks = slice(h * GLA_DK, (h + 1) * GLA_DK)
        vs = slice(h * GLA_DV, (h + 1) * GLA_DV)
        q = q_ref[:, ks].astype(F32) * (GLA_DK ** -0.5)
        k = k_ref[:, ks].astype(F32)
        v = v_ref[:, vs]
        s = _dot_nt((q * e_q[:, ks]).astype(BF16), (k * e_k[:, ks]).astype(BF16))
        s = jnp.where(incl, s, 0.0)
        st = st_ref[h]
        o = _dot(s.astype(BF16), v) + _dot_nt((q * e_in[:, ks]).astype(BF16), st.astype(BF16))
        st_ref[h] = st * e_tot[:, ks] + _dot_tn(v, (k * e_out[:, ks]).astype(BF16))
        o_ref[0, :, vs] = o


def gla_scan(p, paa, wa_blk, ba_blk, *, nb, rows_per_batch, n_ctx_chunks, nh):
    t = p.shape[0]
    c = GLA_CHUNK
    ncb = rows_per_batch // c
    kw = nh * GLA_DK
    vw = nh * GLA_DV

    def chunk(b, d, g):
        bwd = jnp.where(g < n_ctx_chunks, n_ctx_chunks - 1 - g, ncb - 1 + n_ctx_chunks - g)
        return b * ncb + jnp.where(d == 0, g, bwd)

    return pl.pallas_call(
        functools.partial(_gla_kernel, nh=nh),
        out_shape=jax.ShapeDtypeStruct((2, t, vw), F32),
        grid=(nb, 2, ncb),
        in_specs=[
            pl.BlockSpec((c, kw), lambda b, d, g: (chunk(b, d, g), 0)),
            pl.BlockSpec((c, kw), lambda b, d, g: (chunk(b, d, g), 1)),
            pl.BlockSpec((c, vw), lambda b, d, g: (chunk(b, d, g), 2 * kw // vw)),
            pl.BlockSpec((c, LANES), lambda b, d, g: (chunk(b, d, g), 0)),
            pl.BlockSpec((1, LANES, kw), lambda b, d, g: (d, 0, 0)),
            pl.BlockSpec((1, 1, kw), lambda b, d, g: (d, 0, 0)),
        ],
        out_specs=pl.BlockSpec((1, c, vw), lambda b, d, g: (d, chunk(b, d, g), 0)),
        scratch_shapes=[pltpu.VMEM((nh, GLA_DV, GLA_DK), F32)],
        compiler_params=_cparams(("arbitrary", "arbitrary", "arbitrary")),
        name="gla_scan",
    )(p, p, p, paa, wa_blk, ba_blk)


def _gla_out_kernel(o_ref, gate_ref, gn_ref, out_ref, *, nh):
    o = o_ref[0] + o_ref[1]
    for h in range(nh):
        vs = slice(h * GLA_DV, (h + 1) * GLA_DV)
        gate = gate_ref[:, vs].astype(F32)
        out_ref[:, vs] = (_rms(o[:, vs], gn_ref[...]) * _silu(gate)).astype(out_ref.dtype)


def gla_output(o2, p, gn, *, gate_block, nh):
    t = p.shape[0]
    vw = nh * GLA_DV
    return pl.pallas_call(
        functools.partial(_gla_out_kernel, nh=nh),
        out_shape=jax.ShapeDtypeStruct((t, vw), BF16),
        grid=(t // ROW_TILE,),
        in_specs=[
            pl.BlockSpec((2, ROW_TILE, vw), lambda i: (0, i, 0)),
            pl.BlockSpec((ROW_TILE, vw), lambda i: (i, gate_block)),
            pl.BlockSpec((1, GLA_DV), lambda i: (0, 0)),
        ],
        out_specs=pl.BlockSpec((ROW_TILE, vw), lambda i: (i, 0)),
        compiler_params=_cparams(("arbitrary",)),
        name="gla_output",
    )(o2, p, gn.reshape(1, GLA_DV))


def _rope(x, cos, sin_signed):
    lane = lax.broadcasted_iota(jnp.int32, x.shape, 1)
    first = (lane % (HEAD_DIM // 2)) < (HEAD_DIM // 4)
    rot = jnp.where(first, pltpu.roll(x, HEAD_DIM - HEAD_DIM // 4, 1), pltpu.roll(x, HEAD_DIM // 4, 1))
    return x * cos + rot * sin_signed


def rope_tables(n_ctx, seq):
    nf = HEAD_DIM // 4
    rows = seq // GRID_W
    row = jnp.repeat(jnp.arange(rows, dtype=jnp.int32), GRID_W).astype(F32)
    col = jnp.tile(jnp.arange(GRID_W, dtype=jnp.int32), rows).astype(F32)
    inv = ROPE_THETA ** (-jnp.arange(nf, dtype=F32) / nf)
    ang = jnp.concatenate([row[:, None] * inv, row[:, None] * inv, col[:, None] * inv, col[:, None] * inv], axis=1)
    sign = jnp.tile(jnp.concatenate([-jnp.ones((nf,), F32), jnp.ones((nf,), F32)]), 2)
    cos = jnp.concatenate([jnp.ones((n_ctx, HEAD_DIM), F32), jnp.cos(ang)], axis=0)
    sin = jnp.concatenate([jnp.zeros((n_ctx, HEAD_DIM), F32), jnp.sin(ang) * sign], axis=0)
    return cos, sin


def _softmax_parts(s):
    m = jnp.max(s, axis=-1, keepdims=True)
    e = jnp.exp(s - m)
    return e, jnp.sum(e, axis=-1, keepdims=True)


def _gqa_kernel(q_ref, k_ref, v_ref, cq_ref, sq_ref, ck_ref, sk_ref, qn_ref, kn_ref, o_ref, ks_ref, *,
                n_ctx, ctx_tile_first):
    qi = pl.program_id(2)

    @pl.when(qi == 0)
    def _():
        k = _rms(k_ref[...].astype(F32), kn_ref[...])
        ks_ref[...] = _rope(k, ck_ref[...], sk_ref[...]).astype(BF16)

    def attend(nk):
        keys = ks_ref[0:nk, :]
        vals = v_ref[0:nk, :]
        for g in range(GQA_GROUP):
            hs = slice(g * HEAD_DIM, (g + 1) * HEAD_DIM)
            q = _rms(q_ref[:, hs].astype(F32), qn_ref[...])
            q = (_rope(q, cq_ref[...], sq_ref[...]) * (HEAD_DIM ** -0.5)).astype(BF16)
            e, l = _softmax_parts(_dot_nt(q, keys))
            o_ref[:, hs] = (_dot(e.astype(BF16), vals) / l).astype(o_ref.dtype)

    if ctx_tile_first:
        @pl.when(qi == 0)
        def _():
            attend(n_ctx)

        @pl.when(qi > 0)
        def _():
            attend(k_ref.shape[0])
    else:
        attend(k_ref.shape[0])


def gqa_attention(p, cos, sin, qn, kn, *, nb, rows_per_batch, n_ctx, n_kv, q_col, k_col, v_col, with_ctx):
    t = p.shape[0]
    tpb = rows_per_batch // ROW_TILE
    nq = tpb if with_ctx else tpb - 1
    off = 0 if with_ctx else 1
    qw = GQA_GROUP * HEAD_DIM
    return pl.pallas_call(
        functools.partial(_gqa_kernel, n_ctx=n_ctx, ctx_tile_first=with_ctx),
        out_shape=jax.ShapeDtypeStruct((t, n_kv * qw), BF16),
        grid=(nb, n_kv, nq),
        in_specs=[
            pl.BlockSpec((ROW_TILE, qw), lambda b, h, i: (b * tpb + i + off, q_col // qw + h)),
            pl.BlockSpec((rows_per_batch, HEAD_DIM), lambda b, h, i: (b, k_col // HEAD_DIM + h)),
            pl.BlockSpec((rows_per_batch, HEAD_DIM), lambda b, h, i: (b, v_col // HEAD_DIM + h)),
            pl.BlockSpec((ROW_TILE, HEAD_DIM), lambda b, h, i: (i + off, 0)),
            pl.BlockSpec((ROW_TILE, HEAD_DIM), lambda b, h, i: (i + off, 0)),
            pl.BlockSpec((rows_per_batch, HEAD_DIM), lambda b, h, i: (0, 0)),
            pl.BlockSpec((rows_per_batch, HEAD_DIM), lambda b, h, i: (0, 0)),
            pl.BlockSpec((1, HEAD_DIM), lambda b, h, i: (0, 0)),
            pl.BlockSpec((1, HEAD_DIM), lambda b, h, i: (0, 0)),
        ],
        out_specs=pl.BlockSpec((ROW_TILE, qw), lambda b, h, i: (b * tpb + i + off, h)),
        scratch_shapes=[pltpu.VMEM((rows_per_batch, HEAD_DIM), BF16)],
        compiler_params=_cparams(("arbitrary", "arbitrary", "arbitrary")),
        name="gqa_attention",
    )(p, p, p, cos, sin, cos, sin, qn.reshape(1, HEAD_DIM), kn.reshape(1, HEAD_DIM))


def _diff_kernel(q_ref, k_ref, v_ref, cq_ref, sq_ref, ck_ref, sk_ref, lq1_ref, lk1_ref, lq2_ref, lk2_ref, gn_ref,
                 o_ref, ks_ref, *, n_ctx, ctx_tile_first, lam_init):
    qi = pl.program_id(2)

    @pl.when(qi == 0)
    def _():
        for j in range(2):
            hs = slice(j * HEAD_DIM, (j + 1) * HEAD_DIM)
            ks_ref[:, hs] = _rope(k_ref[:, hs].astype(F32), ck_ref[...], sk_ref[...]).astype(BF16)

    lam = (jnp.exp(jnp.sum(lq1_ref[...] * lk1_ref[...], axis=-1, keepdims=True))
           - jnp.exp(jnp.sum(lq2_ref[...] * lk2_ref[...], axis=-1, keepdims=True)) + lam_init)

    def attend(nk):
        w = None
        for j in range(2):
            hs = slice(j * HEAD_DIM, (j + 1) * HEAD_DIM)
            q = (_rope(q_ref[:, hs].astype(F32), cq_ref[...], sq_ref[...]) * (HEAD_DIM ** -0.5)).astype(BF16)
            e, l = _softmax_parts(_dot_nt(q, ks_ref[0:nk, hs]))
            pj = e / l
            w = pj if j == 0 else w - lam * pj
        o = _dot(w.astype(BF16), v_ref[0:nk, :])
        o_ref[...] = (_rms(o, gn_ref[...]) * (1.0 - lam_init)).astype(o_ref.dtype)

    if ctx_tile_first:
        @pl.when(qi == 0)
        def _():
            attend(n_ctx)

        @pl.when(qi > 0)
        def _():
            attend(k_ref.shape[0])
    else:
        attend(k_ref.shape[0])


def diff_attention(p, cos, sin, lq1, lk1, lq2, lk2, gn, *, nb, rows_per_batch, n_ctx, nh, q_col, k_col, v_col,
                   with_ctx, lam_init):
    t = p.shape[0]
    tpb = rows_per_batch // ROW_TILE
    nq = tpb if with_ctx else tpb - 1
    off = 0 if with_ctx else 1
    w2 = 2 * HEAD_DIM
    vec = pl.BlockSpec((1, HEAD_DIM), lambda b, h, i: (0, 0))
    return pl.pallas_call(
        functools.partial(_diff_kernel, n_ctx=n_ctx, ctx_tile_first=with_ctx, lam_init=lam_init),
        out_shape=jax.ShapeDtypeStruct((t, nh * DIFF_DV), BF16),
        grid=(nb, nh, nq),
        in_specs=[
            pl.BlockSpec((ROW_TILE, w2), lambda b, h, i: (b * tpb + i + off, q_col // w2 + h)),
            pl.BlockSpec((rows_per_batch, w2), lambda b, h, i: (b, k_col // w2 + h)),
            pl.BlockSpec((rows_per_batch, DIFF_DV), lambda b, h, i: (b, v_col // DIFF_DV + h)),
            pl.BlockSpec((ROW_TILE, HEAD_DIM), lambda b, h, i: (i + off, 0)),
            pl.BlockSpec((ROW_TILE, HEAD_DIM), lambda b, h, i: (i + off, 0)),
            pl.BlockSpec((rows_per_batch, HEAD_DIM), lambda b, h, i: (0, 0)),
            pl.BlockSpec((rows_per_batch, HEAD_DIM), lambda b, h, i: (0, 0)),
            vec, vec, vec, vec,
            pl.BlockSpec((1, DIFF_DV), lambda b, h, i: (0, 0)),
        ],
        out_specs=pl.BlockSpec((ROW_TILE, DIFF_DV), lambda b, h, i: (b * tpb + i + off, h)),
        scratch_shapes=[pltpu.VMEM((rows_per_batch, w2), BF16)],
        compiler_params=_cparams(("arbitrary", "arbitrary", "arbitrary")),
        name="diff_attention",
    )(p, p, p, cos, sin, cos, sin, lq1.reshape(1, -1), lk1.reshape(1, -1), lq2.reshape(1, -1), lk2.reshape(1, -1),
      gn.reshape(1, DIFF_DV))


def _route_kernel(lg_ref, bias_ref, o_ref):
    x = lg_ref[...] + bias_ref[...]
    lane = lax.broadcasted_iota(jnp.int32, x.shape, 1).astype(F32)
    neg = -jnp.inf

    def first_max(vals, mask):
        v = jnp.where(mask, vals, neg)
        m = jnp.max(v, axis=-1, keepdims=True)
        idx = jnp.min(jnp.where(mask & (v == m), lane, float(LANES)), axis=-1, keepdims=True)
        return m, idx

    gmask = lane < N_GROUPS
    gm, gidx = first_max(x, gmask)
    g_w = 1.0 / jnp.sum(jnp.where(gmask, jnp.exp(x - gm), 0.0), axis=-1, keepdims=True)
    lo = N_GROUPS + EXPERTS_PER_GROUP * gidx
    emask = (lane >= lo) & (lane < lo + EXPERTS_PER_GROUP)
    m1, i1 = first_max(x, emask)
    m2, i2 = first_max(x, emask & (lane != i1))
    r = jnp.exp(m2 - m1)
    w1 = g_w / (1.0 + r)
    w2 = g_w * r / (1.0 + r)
    out = jnp.where(lane == 0, i1 - N_GROUPS, 0.0)
    out = jnp.where(lane == 1, i2 - N_GROUPS, out)
    out = jnp.where(lane == 2, w1, out)
    out = jnp.where(lane == 3, w2, out)
    o_ref[...] = out


def route(logits, bias):
    t = logits.shape[0]
    return pl.pallas_call(
        _route_kernel,
        out_shape=jax.ShapeDtypeStruct((t, LANES), F32),
        grid=(t // ROW_TILE,),
        in_specs=[pl.BlockSpec((ROW_TILE, LANES), lambda i: (i, 0)), pl.BlockSpec((1, LANES), lambda i: (0, 0))],
        out_specs=pl.BlockSpec((ROW_TILE, LANES), lambda i: (i, 0)),
        compiler_params=_cparams(("arbitrary",)),
        name="route",
    )(logits, bias)


def dispatch_plan(ids, n_tiles):
    t = ids.shape[0]
    e = ids.reshape(-1)
    onehot = (e[:, None] == jnp.arange(N_EXPERTS, dtype=jnp.int32)[None, :]).astype(jnp.int32)
    csum = jnp.cumsum(onehot, axis=0)
    counts = csum[-1]
    tiles_per = (counts + MOE_TILE - 1) // MOE_TILE
    tile_end = jnp.cumsum(tiles_per)
    row_start = (tile_end - tiles_per) * MOE_TILE
    pos = jnp.sum(onehot * (csum - 1 + row_start[None, :]), axis=1)
    src = jnp.zeros((n_tiles * MOE_TILE,), jnp.int32).at[pos].set(jnp.arange(2 * t, dtype=jnp.int32) // 2)
    tile_ids = jnp.arange(n_tiles, dtype=jnp.int32)
    tile_expert = jnp.minimum(jnp.sum((tile_end[None, :] <= tile_ids[:, None]).astype(jnp.int32), axis=1),
                              N_EXPERTS - 1)
    return src, pos.reshape(t, 2), tile_expert, tile_end[-1:]


def _row_copy(src_hbm, idx, dst_vmem, r, sem):
    return pltpu.make_async_copy(src_hbm.at[pl.ds(idx, 1)], dst_vmem.at[pl.ds(r, 1)], sem)


def _expert_kernel(te_ref, nused_ref, src_ref, h_hbm, wg_hbm, wu_hbm, wd_hbm, y_ref,
                   xbuf, wg_s, wu_s, wd_s, st_in, st_out, sem_x, sem_w, *, expert_base):
    j = pl.program_id(0)
    n_used = nused_ref[0]
    rows = y_ref.shape[0]
    d, f = wg_s.shape
    cg, cd = d // W_CHUNKS, f // W_CHUNKS

    def start_rows(tile, slot):
        def body(r, carry):
            _row_copy(h_hbm, src_ref[tile * rows + r], xbuf.at[slot], r, sem_x.at[slot]).start()
            return carry

        lax.fori_loop(0, rows, body, 0)

    def weight_copy(e, i, slot):
        kind, c = divmod(i, W_CHUNKS)
        if kind < 2:
            w_hbm = wg_hbm if kind == 0 else wu_hbm
            return pltpu.make_async_copy(w_hbm.at[e, pl.ds(c * cg, cg)], st_in.at[slot], sem_w.at[slot])
        return pltpu.make_async_copy(wd_hbm.at[e, pl.ds(c * cd, cd)], st_out.at[slot], sem_w.at[slot])

    def load_weights(e):
        n = 3 * W_CHUNKS
        weight_copy(e, 0, 0).start()
        for i in range(n):
            slot = i % 2
            if i + 1 < n:
                weight_copy(e, i + 1, 1 - slot).start()
            weight_copy(e, i, slot).wait()
            kind, c = divmod(i, W_CHUNKS)
            if kind == 0:
                wg_s[c * cg:(c + 1) * cg, :] = st_in[slot].astype(BF16)
            elif kind == 1:
                wu_s[c * cg:(c + 1) * cg, :] = st_in[slot].astype(BF16)
            else:
                wd_s[c * cd:(c + 1) * cd, :] = st_out[slot].astype(BF16)

    @pl.when(j == 0)
    def _():
        start_rows(0, 0)

    @pl.when(j < n_used)
    def _():
        slot = j % 2

        @pl.when(j + 1 < n_used)
        def _():
            start_rows(j + 1, 1 - slot)

        e = te_ref[j]

        @pl.when((j == 0) | (te_ref[jnp.maximum(j - 1, 0)] != e))
        def _():
            load_weights(expert_base + e)

        pltpu.make_async_copy(h_hbm.at[pl.ds(0, rows)], xbuf.at[slot], sem_x.at[slot]).wait()
        x = xbuf[slot].astype(BF16)
        hid = _silu(_dot(x, wg_s[...])) * _dot(x, wu_s[...])
        y_ref[...] = _dot(hid.astype(BF16), wd_s[...])

    @pl.when(j >= n_used)
    def _():
        y_ref[...] = jnp.zeros_like(y_ref)


def expert_mlp(h, src, tile_expert, n_used, wg, wu, wd, *, expert_base):
    d = h.shape[1]
    f = wg.shape[2]
    n_rows = src.shape[0]
    return pl.pallas_call(
        functools.partial(_expert_kernel, expert_base=expert_base),
        out_shape=jax.ShapeDtypeStruct((n_rows, d), F32),
        grid_spec=pltpu.PrefetchScalarGridSpec(
            num_scalar_prefetch=3,
            grid=(n_rows // MOE_TILE,),
            in_specs=[pl.BlockSpec(memory_space=pl.ANY)] * 4,
            out_specs=pl.BlockSpec((MOE_TILE, d), lambda j, te, n, s: (j, 0)),
            scratch_shapes=[
                pltpu.VMEM((2, MOE_TILE, d), F32),
                pltpu.VMEM((d, f), BF16), pltpu.VMEM((d, f), BF16), pltpu.VMEM((f, d), BF16),
                pltpu.VMEM((2, d // W_CHUNKS, f), F32), pltpu.VMEM((2, f // W_CHUNKS, d), F32),
                pltpu.SemaphoreType.DMA((2,)), pltpu.SemaphoreType.DMA((2,)),
            ],
        ),
        compiler_params=_cparams(("arbitrary",)),
        name="moe_experts",
    )(tile_expert, n_used, src, h, wg, wu, wd)


def _combine_kernel(p1_ref, p2_ref, x_ref, g_ref, r_ref, y_hbm, o_ref, buf_ref, sem):
    i = pl.program_id(0)
    rows = x_ref.shape[0]

    def start(r, carry):
        _row_copy(y_hbm, p1_ref[i * rows + r], buf_ref.at[0], r, sem).start()
        _row_copy(y_hbm, p2_ref[i * rows + r], buf_ref.at[1], r, sem).start()
        return carry

    lax.fori_loop(0, rows, start, 0)
    pltpu.make_async_copy(y_hbm.at[pl.ds(0, rows)], buf_ref.at[0], sem).wait()
    pltpu.make_async_copy(y_hbm.at[pl.ds(0, rows)], buf_ref.at[1], sem).wait()
    w1 = r_ref[:, 2:3]
    w2 = r_ref[:, 3:4]
    o_ref[...] = x_ref[...] + g_ref[0] * (w1 * buf_ref[0] + w2 * buf_ref[1])


def moe_combine(x, y, pos, routed, mod3, gate_chunk, mod_row):
    t, d = x.shape
    return pl.pallas_call(
        _combine_kernel,
        out_shape=jax.ShapeDtypeStruct((t, d), F32),
        grid_spec=pltpu.PrefetchScalarGridSpec(
            num_scalar_prefetch=2,
            grid=(t // ROW_TILE,),
            in_specs=[
                pl.BlockSpec((ROW_TILE, d), lambda i, a, b: (i, 0)),
                pl.BlockSpec((1, 1, d), lambda i, a, b: (mod_row(i), 0, gate_chunk)),
                pl.BlockSpec((ROW_TILE, LANES), lambda i, a, b: (i, 0)),
                pl.BlockSpec(memory_space=pl.ANY),
            ],
            out_specs=pl.BlockSpec((ROW_TILE, d), lambda i, a, b: (i, 0)),
            scratch_shapes=[pltpu.VMEM((2, ROW_TILE, d), F32), pltpu.SemaphoreType.DMA],
        ),
        compiler_params=_cparams(("arbitrary",)),
        name="moe_combine",
    )(pos[:, 0], pos[:, 1], x, mod3, routed, y)


class _LatentTiles:
    def __init__(self, nb, tiles_per_batch):
        self.n_out = nb * (tiles_per_batch - 1)
        self._lat = tiles_per_batch - 1
        self._tpb = tiles_per_batch

    def __call__(self, i):
        return (i // self._lat) * self._tpb + 1 + i % self._lat


def kernel(x, c, ctx, c_ctx, ada_w, ada_b, norm1_g, w_in, gla_wa_fwd, gla_ba_fwd, gla_wa_bwd, gla_ba_bwd, gla_norm_g,
           gqa_qnorm_g, gqa_knorm_g, diff_lq1, diff_lk1, diff_lq2, diff_lk2, diff_norm_g, w_out, norm2_g, router_wg,
           router_bg, router_we, router_be, moe_w_gate, moe_w_up, moe_w_down, final_norm_g):
    nb, seq, d = x.shape
    n_ctx = ctx.shape[1]
    depth = ada_w.shape[0]
    assert n_ctx == ROW_TILE and seq % ROW_TILE == 0 and seq % GRID_W == 0
    rpb = n_ctx + seq
    tpb = rpb // ROW_TILE
    t = nb * rpb

    gla_heads = (d // 4) // GLA_DV
    gqa_heads = (d // 2) // HEAD_DIM
    n_kv = gqa_heads // GQA_GROUP
    diff_heads = (d // 4) // DIFF_DV
    kw = gla_heads * GLA_DK
    vw = gla_heads * GLA_DV
    splits = (kw, kw, vw, vw, GLA_RANK, GLA_RANK, gqa_heads * HEAD_DIM, n_kv * HEAD_DIM, n_kv * HEAD_DIM,
              diff_heads * 2 * HEAD_DIM, diff_heads * 2 * HEAD_DIM, diff_heads * DIFF_DV)
    offs = [0]
    for s in splits:
        offs.append(offs[-1] + s)
    wide_cols = jnp.concatenate([jnp.arange(offs[0], offs[4]), jnp.arange(offs[6], offs[12])])
    col = {}
    acc = 0
    for name, width in (("aq", kw), ("ak", kw), ("av", vw), ("ag", vw), ("bq", splits[6]), ("bk", splits[7]),
                        ("bv", splits[8]), ("dq", splits[9]), ("dk", splits[10]), ("dv", splits[11])):
        col[name] = acc
        acc += width
    wide = acc

    def mod_row(i):
        return jnp.where(i % tpb == 0, nb, i // tpb)

    tokens = jnp.concatenate([ctx, x], axis=1).reshape(t, d)
    c_all = jnp.zeros((8, d), F32).at[:nb].set(c).at[nb].set(c_ctx)
    mod = ada_modulation(c_all, ada_w, ada_b)
    cos, sin = rope_tables(n_ctx, seq)
    n_tiles = (2 * t) // MOE_TILE + N_EXPERTS
    f = moe_w_gate.shape[-1]
    wg_all = moe_w_gate.reshape(depth * N_EXPERTS, d, f)
    wu_all = moe_w_up.reshape(depth * N_EXPERTS, d, f)
    wd_all = moe_w_down.reshape(depth * N_EXPERTS, f, d)

    for l in range(depth):
        last = l == depth - 1
        lam_init = 0.8 - 0.6 * math.exp(-0.3 * l)
        mod3 = mod[l].reshape(8, 1, 6 * d)

        w_wide = w_in[l][:, wide_cols].astype(BF16)
        w_dec = jnp.zeros((d, LANES), F32).at[:, :2 * GLA_RANK].set(w_in[l][:, offs[4]:offs[6]])
        h, paa = norm_modulate(tokens, norm1_g[l], mod3, 1, 0, w_dec, mod_row, h_dtype=BF16, exact_small=False)
        p = matmul(h, w_wide, bm=1024, bn=1024, out_dtype=BF16)

        wa_blk = jnp.zeros((2, LANES, kw), F32)
        wa_blk = wa_blk.at[0, :GLA_RANK].set(gla_wa_fwd[l]).at[1, GLA_RANK:2 * GLA_RANK].set(gla_wa_bwd[l])
        ba_blk = jnp.stack([gla_ba_fwd[l], gla_ba_bwd[l]]).reshape(2, 1, kw)
        o2 = gla_scan(p, paa, wa_blk, ba_blk, nb=nb, rows_per_batch=rpb, n_ctx_chunks=n_ctx // GLA_CHUNK,
                      nh=gla_heads)
        mix_a = gla_output(o2, p, gla_norm_g[l], gate_block=col["ag"] // vw, nh=gla_heads)
        mix_b = gqa_attention(p, cos, sin, gqa_qnorm_g[l], gqa_knorm_g[l], nb=nb, rows_per_batch=rpb, n_ctx=n_ctx,
                              n_kv=n_kv, q_col=col["bq"], k_col=col["bk"], v_col=col["bv"], with_ctx=True)
        mix_c = diff_attention(p, cos, sin, diff_lq1[l], diff_lk1[l], diff_lq2[l], diff_lk2[l], diff_norm_g[l],
                               nb=nb, rows_per_batch=rpb, n_ctx=n_ctx, nh=diff_heads, q_col=col["dq"],
                               k_col=col["dk"], v_col=col["dv"], with_ctx=True, lam_init=lam_init)
        mix = jnp.concatenate([mix_a, mix_b, mix_c], axis=1)
        tokens = matmul_gated_residual(mix, w_out[l].astype(BF16), tokens, mod3, 2, mod_row, bn=1024)

        w_route = jnp.zeros((d, LANES), F32).at[:, :N_GROUPS].set(router_wg[l])
        w_route = w_route.at[:, N_GROUPS:N_GROUPS + N_EXPERTS].set(router_we[l])
        b_route = jnp.zeros((1, LANES), F32).at[0, :N_GROUPS].set(router_bg[l])
        b_route = b_route.at[0, N_GROUPS:N_GROUPS + N_EXPERTS].set(router_be[l])
        h2, logits = norm_modulate(tokens, norm2_g[l], mod3, 4, 3, w_route, mod_row, h_dtype=F32, exact_small=True)
        routed = route(logits, b_route)
        src, pos, tile_expert, n_used = dispatch_plan(routed[:, 0:2].astype(jnp.int32), n_tiles)
        y = expert_mlp(h2, src, tile_expert, n_used, wg_all, wu_all, wd_all, expert_base=l * N_EXPERTS)
        tokens = moe_combine(tokens, y, pos, routed, mod3, 5, mod_row)

    out = final_norm(tokens, final_norm_g, _LatentTiles(nb, tpb))
    return out.reshape(nb, seq, d)
```

```python
import functools
import math

import jax
import jax.numpy as jnp
from jax import lax
from jax.experimental import pallas as pl
from jax.experimental.pallas import tpu as pltpu

F32 = jnp.float32
BF16 = jnp.bfloat16

HEAD_DIM = 128
GRID_W = 64
ROPE_THETA = 10000.0
EPS = 1e-6
GLA_DK = HEAD_DIM
GLA_DV = 2 * HEAD_DIM
GLA_RANK = 16
GLA_TAU = 16.0
GLA_CHUNK = 64
GQA_GROUP = 4
DIFF_DV = 2 * HEAD_DIM
N_GROUPS = 4
EXPERTS_PER_GROUP = 8
N_EXPERTS = N_GROUPS * EXPERTS_PER_GROUP

LANES = 128
ROW_TILE = 256
SEG_TILE = 128
SEG_ROWS = 768
W_CHUNKS = 8
W_SLOTS = 2
VMEM_LIMIT = 56 * 1024 * 1024


def _cparams(sem, vmem=VMEM_LIMIT):
    return pltpu.CompilerParams(dimension_semantics=sem, vmem_limit_bytes=vmem)


def _sigmoid(x):
    return 1.0 / (1.0 + jnp.exp(-x))


def _silu(x):
    return x * _sigmoid(x)


def _log_sigmoid(x):
    return jnp.minimum(x, 0.0) - jnp.log(1.0 + jnp.exp(-jnp.abs(x)))


def _rms(x, g):
    return x * lax.rsqrt(jnp.mean(x * x, axis=-1, keepdims=True) + EPS) * g


def _dot(a, b):
    return jnp.dot(a, b, preferred_element_type=F32)


def _dot_nt(a, b):
    return lax.dot_general(a, b, (((1,), (1,)), ((), ())), preferred_element_type=F32)


def _dot_tn(a, b):
    return lax.dot_general(a, b, (((0,), (0,)), ((), ())), preferred_element_type=F32)


class _AllTiles:
    def __init__(self, n):
        self.n = n

    def __call__(self, i):
        return i


class _LatentTiles:
    def __init__(self, nb, tiles_per_batch):
        self.n = nb * (tiles_per_batch - 1)
        self._lat = tiles_per_batch - 1
        self._tpb = tiles_per_batch

    def __call__(self, i):
        return (i // self._lat) * self._tpb + 1 + i % self._lat


def _ada_kernel(c_ref, w_ref, b_ref, o_ref):
    s = _silu(c_ref[...]).astype(BF16)
    o_ref[0] = _dot(s, w_ref[0].astype(BF16)) + b_ref[0]


def ada_modulation(c_all, ada_w, ada_b, *, tn=512):
    nl, d, n = ada_w.shape
    return pl.pallas_call(
        _ada_kernel,
        out_shape=jax.ShapeDtypeStruct((nl, 8, n), F32),
        grid=(nl, n // tn),
        in_specs=[
            pl.BlockSpec((8, d), lambda l, j: (0, 0)),
            pl.BlockSpec((1, d, tn), lambda l, j: (l, 0, j)),
            pl.BlockSpec((1, 1, tn), lambda l, j: (l, 0, j)),
        ],
        out_specs=pl.BlockSpec((1, 8, tn), lambda l, j: (l, 0, j)),
        compiler_params=_cparams(("arbitrary", "arbitrary")),
        name="ada_modulation",
    )(c_all, ada_w, ada_b.reshape(nl, 1, n))


def _norm_kernel(x_ref, g_ref, sc_ref, sh_ref, w_ref, h_ref, s_ref, *, exact_small):
    h = _rms(x_ref[...], g_ref[...]) * (1.0 + sc_ref[0]) + sh_ref[0]
    h_ref[...] = h.astype(h_ref.dtype)
    if exact_small:
        s_ref[...] = jnp.dot(h, w_ref[...], precision=lax.Precision.HIGHEST, preferred_element_type=F32)
    else:
        s_ref[...] = _dot(h.astype(BF16), w_ref[...].astype(BF16))


def norm_modulate(x, g, mod3, sc_chunk, sh_chunk, w_small, mod_row, *, h_dtype, exact_small):
    t, d = x.shape
    ns = w_small.shape[1]
    return pl.pallas_call(
        functools.partial(_norm_kernel, exact_small=exact_small),
        out_shape=(jax.ShapeDtypeStruct((t, d), h_dtype), jax.ShapeDtypeStruct((t, ns), F32)),
        grid=(t // ROW_TILE,),
        in_specs=[
            pl.BlockSpec((ROW_TILE, d), lambda i: (i, 0)),
            pl.BlockSpec((1, d), lambda i: (0, 0)),
            pl.BlockSpec((1, 1, d), lambda i: (mod_row(i), 0, sc_chunk)),
            pl.BlockSpec((1, 1, d), lambda i: (mod_row(i), 0, sh_chunk)),
            pl.BlockSpec((d, ns), lambda i: (0, 0)),
        ],
        out_specs=(pl.BlockSpec((ROW_TILE, d), lambda i: (i, 0)), pl.BlockSpec((ROW_TILE, ns), lambda i: (i, 0))),
        compiler_params=_cparams(("arbitrary",)),
        name="norm_modulate",
    )(x, g.reshape(1, d), mod3, mod3, w_small)


def _final_norm_kernel(x_ref, g_ref, o_ref):
    o_ref[...] = _rms(x_ref[...], g_ref[...])


def final_norm(x, g):
    t, d = x.shape
    return pl.pallas_call(
        _final_norm_kernel,
        out_shape=jax.ShapeDtypeStruct((t, d), F32),
        grid=(t // ROW_TILE,),
        in_specs=[pl.BlockSpec((ROW_TILE, d), lambda i: (i, 0)), pl.BlockSpec((1, d), lambda i: (0, 0))],
        out_specs=pl.BlockSpec((ROW_TILE, d), lambda i: (i, 0)),
        compiler_params=_cparams(("arbitrary",)),
        name="final_norm",
    )(x, g.reshape(1, d))


def _mm_kernel(a_ref, b_ref, o_ref):
    o_ref[...] = _dot(a_ref[...], b_ref[...]).astype(o_ref.dtype)


def matmul(a, b_all, layer, *, bm, bn, out_dtype):
    m, k = a.shape
    n = b_all.shape[2]
    return pl.pallas_call(
        _mm_kernel,
        out_shape=jax.ShapeDtypeStruct((m, n), out_dtype),
        grid=(m // bm, n // bn),
        in_specs=[pl.BlockSpec((bm, k), lambda i, j: (i, 0)), pl.BlockSpec((None, k, bn), lambda i, j: (layer, 0, j))],
        out_specs=pl.BlockSpec((bm, bn), lambda i, j: (i, j)),
        compiler_params=_cparams(("arbitrary", "arbitrary")),
        name="matmul",
    )(a, b_all)


def _mm3_res_kernel(a1_ref, a2_ref, a3_ref, b_ref, x_ref, g_ref, o_ref):
    k1 = a1_ref.shape[1]
    k2 = k1 + a2_ref.shape[1]
    acc = _dot(a1_ref[...], b_ref[0:k1, :]) + _dot(a2_ref[...], b_ref[k1:k2, :]) + _dot(a3_ref[...], b_ref[k2:, :])
    o_ref[...] = x_ref[...] + g_ref[0] * acc


def matmul3_gated_residual(a1, a2, a3, b_all, layer, x, mod3, gate_chunk, x_tile, mod_row, *, bn):
    k = b_all.shape[1]
    n = b_all.shape[2]
    nbn = n // bn
    return pl.pallas_call(
        _mm3_res_kernel,
        out_shape=jax.ShapeDtypeStruct((x_tile.n * ROW_TILE, n), F32),
        grid=(nbn, x_tile.n),
        in_specs=[
            pl.BlockSpec((ROW_TILE, a1.shape[1]), lambda j, i: (i, 0)),
            pl.BlockSpec((ROW_TILE, a2.shape[1]), lambda j, i: (i, 0)),
            pl.BlockSpec((ROW_TILE, a3.shape[1]), lambda j, i: (i, 0)),
            pl.BlockSpec((None, k, bn), lambda j, i: (layer, 0, j)),
            pl.BlockSpec((ROW_TILE, bn), lambda j, i: (x_tile(i), j)),
            pl.BlockSpec((1, 1, bn), lambda j, i: (mod_row(i), 0, gate_chunk * nbn + j)),
        ],
        out_specs=pl.BlockSpec((ROW_TILE, bn), lambda j, i: (i, j)),
        compiler_params=_cparams(("arbitrary", "arbitrary")),
        name="matmul3_gated_residual",
    )(a1, a2, a3, b_all, x, mod3)


def _gla_direction(q_ref, k_ref, v_ref, paa_ref, wa, ba, o_ref, st_ref, *, nh, backward):
    c = GLA_CHUNK
    row = lax.broadcasted_iota(jnp.int32, (c, c), 0)
    col = lax.broadcasted_iota(jnp.int32, (c, c), 1)
    incl = (col >= row) if backward else (col <= row)
    tri = incl.astype(BF16)

    z = _dot(paa_ref[...].astype(BF16), wa.astype(BF16)) + ba
    la = _log_sigmoid(z) * (1.0 / GLA_TAU)
    la_hi = la.astype(BF16)
    la_lo = (la - la_hi.astype(F32)).astype(BF16)
    b = _dot(tri, la_hi) + _dot(tri, la_lo)
    tot = jnp.sum(la, axis=0, keepdims=True)
    mid = 0.5 * tot
    e_q = jnp.exp(b - mid)
    e_k = jnp.exp(mid - b)
    e_in = jnp.exp(b)
    e_out = jnp.exp(tot - b)
    e_tot = jnp.exp(tot)

    for h in range(nh):
        ks = slice(h * GLA_DK, (h + 1) * GLA_DK)
        vs = slice(h * GLA_DV, (h + 1) * GLA_DV)
        q = q_ref[:, ks].astype(F32) * (GLA_DK ** -0.5)
        k = k_ref[:, ks].astype(F32)
        v = v_ref[:, vs]
        s = _dot_nt((q * e_q[:, ks]).astype(BF16), (k * e_k[:, ks]).astype(BF16))
        s = jnp.where(incl, s, 0.0)
        st = st_ref[h]
        o = _dot(s.astype(BF16), v) + _dot_nt((q * e_in[:, ks]).astype(BF16), st.astype(BF16))
        st_ref[h] = st * e_tot[:, ks] + _dot_tn(v, (k * e_out[:, ks]).astype(BF16))
        o_ref[:, vs] = o


def _gla_kernel(qf, kf, vf, pf, qb, kb, vb, pb, wa_ref, ba_ref, of_ref, ob_ref, st_ref, *, nh):
    @pl.when(pl.program_id(1) == 0)
    def _():
        st_ref[...] = jnp.zeros_like(st_ref)

    _gla_direction(qf, kf, vf, pf, wa_ref[0], ba_ref[0], of_ref, st_ref.at[0], nh=nh, backward=False)
    _gla_direction(qb, kb, vb, pb, wa_ref[1], ba_ref[1], ob_ref, st_ref.at[1], nh=nh, backward=True)


def gla_scan(p, paa, wa_blk, ba_blk, *, nb, rows_per_batch, n_ctx_chunks, nh):
    t = p.shape[0]
    c = GLA_CHUNK
    ncb = rows_per_batch // c
    kw = nh * GLA_DK
    vw = nh * GLA_DV

    def fwd(b, g):
        return b * ncb + g

    def bwd(b, g):
        return b * ncb + jnp.where(g < n_ctx_chunks, n_ctx_chunks - 1 - g, ncb - 1 + n_ctx_chunks - g)

    def chunk_specs(cm):
        return [
            pl.BlockSpec((c, kw), lambda b, g: (cm(b, g), 0)),
            pl.BlockSpec((c, kw), lambda b, g: (cm(b, g), 1)),
            pl.BlockSpec((c, vw), lambda b, g: (cm(b, g), 2 * kw // vw)),
            pl.BlockSpec((c, LANES), lambda b, g: (cm(b, g), 0)),
        ]

    out = jax.ShapeDtypeStruct((t, vw), F32)
    return pl.pallas_call(
        functools.partial(_gla_kernel, nh=nh),
        out_shape=(out, out),
        grid=(nb, ncb),
        in_specs=chunk_specs(fwd) + chunk_specs(bwd) + [
            pl.BlockSpec((2, LANES, kw), lambda b, g: (0, 0, 0)),
            pl.BlockSpec((2, 1, kw), lambda b, g: (0, 0, 0)),
        ],
        out_specs=(pl.BlockSpec((c, vw), lambda b, g: (fwd(b, g), 0)), pl.BlockSpec((c, vw), lambda b, g: (bwd(b, g), 0))),
        scratch_shapes=[pltpu.VMEM((2, nh, GLA_DV, GLA_DK), F32)],
        compiler_params=_cparams(("arbitrary", "arbitrary")),
        name="gla_scan",
    )(p, p, p, paa, p, p, p, paa, wa_blk, ba_blk)


def _gla_out_kernel(of_ref, ob_ref, gate_ref, gn_ref, out_ref, *, nh):
    o = of_ref[...] + ob_ref[...]
    for h in range(nh):
        vs = slice(h * GLA_DV, (h + 1) * GLA_DV)
        gate = gate_ref[:, vs].astype(F32)
        out_ref[:, vs] = (_rms(o[:, vs], gn_ref[...]) * _silu(gate)).astype(out_ref.dtype)


def gla_output(o_f, o_b, p, gn, in_tile, *, gate_block, nh):
    vw = nh * GLA_DV
    row = pl.BlockSpec((ROW_TILE, vw), lambda i: (in_tile(i), 0))
    return pl.pallas_call(
        functools.partial(_gla_out_kernel, nh=nh),
        out_shape=jax.ShapeDtypeStruct((in_tile.n * ROW_TILE, vw), BF16),
        grid=(in_tile.n,),
        in_specs=[row, row, pl.BlockSpec((ROW_TILE, vw), lambda i: (in_tile(i), gate_block)),
                  pl.BlockSpec((1, GLA_DV), lambda i: (0, 0))],
        out_specs=pl.BlockSpec((ROW_TILE, vw), lambda i: (i, 0)),
        compiler_params=_cparams(("arbitrary",)),
        name="gla_output",
    )(o_f, o_b, p, gn.reshape(1, GLA_DV))


def _rope(x, cos, sin_signed):
    lane = lax.broadcasted_iota(jnp.int32, x.shape, 1)
    first = (lane % (HEAD_DIM // 2)) < (HEAD_DIM // 4)
    rot = jnp.where(first, pltpu.roll(x, HEAD_DIM - HEAD_DIM // 4, 1), pltpu.roll(x, HEAD_DIM // 4, 1))
    return x * cos + rot * sin_signed


def rope_tables(n_ctx, seq):
    nf = HEAD_DIM // 4
    rows = seq // GRID_W
    row = jnp.repeat(jnp.arange(rows, dtype=jnp.int32), GRID_W).astype(F32)
    col = jnp.tile(jnp.arange(GRID_W, dtype=jnp.int32), rows).astype(F32)
    inv = ROPE_THETA ** (-jnp.arange(nf, dtype=F32) / nf)
    ang = jnp.concatenate([row[:, None] * inv, row[:, None] * inv, col[:, None] * inv, col[:, None] * inv], axis=1)
    sign = jnp.tile(jnp.concatenate([-jnp.ones((nf,), F32), jnp.ones((nf,), F32)]), 2)
    cos = jnp.concatenate([jnp.ones((n_ctx, HEAD_DIM), F32), jnp.cos(ang)], axis=0)
    sin = jnp.concatenate([jnp.zeros((n_ctx, HEAD_DIM), F32), jnp.sin(ang) * sign], axis=0)
    return cos, sin


def _softmax_parts(s):
    m = jnp.max(s, axis=-1, keepdims=True)
    e = jnp.exp(s - m)
    return e, jnp.sum(e, axis=-1, keepdims=True)


def _gqa_kernel(q_ref, k_ref, v_ref, cq_ref, sq_ref, ck_ref, sk_ref, qn_ref, kn_ref, o_ref, ks_ref, *,
                n_ctx, ctx_tile_first):
    qi = pl.program_id(2)

    @pl.when(qi == 0)
    def _():
        k = _rms(k_ref[...].astype(F32), kn_ref[...])
        ks_ref[...] = _rope(k, ck_ref[...], sk_ref[...]).astype(BF16)

    def attend(nk):
        keys = ks_ref[0:nk, :]
        vals = v_ref[0:nk, :]
        for g in range(GQA_GROUP):
            hs = slice(g * HEAD_DIM, (g + 1) * HEAD_DIM)
            q = _rms(q_ref[:, hs].astype(F32), qn_ref[...])
            q = (_rope(q, cq_ref[...], sq_ref[...]) * (HEAD_DIM ** -0.5)).astype(BF16)
            e, l = _softmax_parts(_dot_nt(q, keys))
            o_ref[:, hs] = (_dot(e.astype(BF16), vals) / l).astype(o_ref.dtype)

    if ctx_tile_first:
        @pl.when(qi == 0)
        def _():
            attend(n_ctx)

        @pl.when(qi > 0)
        def _():
            attend(k_ref.shape[0])
    else:
        attend(k_ref.shape[0])


def gqa_attention(p, cos, sin, qn, kn, *, nb, rows_per_batch, n_ctx, n_kv, q_col, k_col, v_col, with_ctx):
    t = p.shape[0]
    tpb = rows_per_batch // ROW_TILE
    nq = tpb if with_ctx else tpb - 1
    off = 0 if with_ctx else 1
    qw = GQA_GROUP * HEAD_DIM
    return pl.pallas_call(
        functools.partial(_gqa_kernel, n_ctx=n_ctx, ctx_tile_first=with_ctx),
        out_shape=jax.ShapeDtypeStruct((nb * nq * ROW_TILE, n_kv * qw), BF16),
        grid=(nb, n_kv, nq),
        in_specs=[
            pl.BlockSpec((ROW_TILE, qw), lambda b, h, i: (b * tpb + i + off, q_col // qw + h)),
            pl.BlockSpec((rows_per_batch, HEAD_DIM), lambda b, h, i: (b, k_col // HEAD_DIM + h)),
            pl.BlockSpec((rows_per_batch, HEAD_DIM), lambda b, h, i: (b, v_col // HEAD_DIM + h)),
            pl.BlockSpec((ROW_TILE, HEAD_DIM), lambda b, h, i: (i + off, 0)),
            pl.BlockSpec((ROW_TILE, HEAD_DIM), lambda b, h, i: (i + off, 0)),
            pl.BlockSpec((rows_per_batch, HEAD_DIM), lambda b, h, i: (0, 0)),
            pl.BlockSpec((rows_per_batch, HEAD_DIM), lambda b, h, i: (0, 0)),
            pl.BlockSpec((1, HEAD_DIM), lambda b, h, i: (0, 0)),
            pl.BlockSpec((1, HEAD_DIM), lambda b, h, i: (0, 0)),
        ],
        out_specs=pl.BlockSpec((ROW_TILE, qw), lambda b, h, i: (b * nq + i, h)),
        scratch_shapes=[pltpu.VMEM((rows_per_batch, HEAD_DIM), BF16)],
        compiler_params=_cparams(("arbitrary", "arbitrary", "arbitrary")),
        name="gqa_attention",
    )(p, p, p, cos, sin, cos, sin, qn.reshape(1, HEAD_DIM), kn.reshape(1, HEAD_DIM))


def _diff_kernel(q_ref, k_ref, v_ref, cq_ref, sq_ref, ck_ref, sk_ref, lq1_ref, lk1_ref, lq2_ref, lk2_ref, gn_ref,
                 o_ref, ks_ref, *, n_ctx, ctx_tile_first, lam_init):
    qi = pl.program_id(2)

    @pl.when(qi == 0)
    def _():
        for j in range(2):
            hs = slice(j * HEAD_DIM, (j + 1) * HEAD_DIM)
            ks_ref[:, hs] = _rope(k_ref[:, hs].astype(F32), ck_ref[...], sk_ref[...]).astype(BF16)

    lam = (jnp.exp(jnp.sum(lq1_ref[...] * lk1_ref[...], axis=-1, keepdims=True))
           - jnp.exp(jnp.sum(lq2_ref[...] * lk2_ref[...], axis=-1, keepdims=True)) + lam_init)

    def attend(nk):
        w = None
        for j in range(2):
            hs = slice(j * HEAD_DIM, (j + 1) * HEAD_DIM)
            q = (_rope(q_ref[:, hs].astype(F32), cq_ref[...], sq_ref[...]) * (HEAD_DIM ** -0.5)).astype(BF16)
            e, l = _softmax_parts(_dot_nt(q, ks_ref[0:nk, hs]))
            pj = e / l
            w = pj if j == 0 else w - lam * pj
        o = _dot(w.astype(BF16), v_ref[0:nk, :])
        o_ref[...] = (_rms(o, gn_ref[...]) * (1.0 - lam_init)).astype(o_ref.dtype)

    if ctx_tile_first:
        @pl.when(qi == 0)
        def _():
            attend(n_ctx)

        @pl.when(qi > 0)
        def _():
            attend(k_ref.shape[0])
    else:
        attend(k_ref.shape[0])


def diff_attention(p, cos, sin, lq1, lk1, lq2, lk2, gn, *, nb, rows_per_batch, n_ctx, nh, q_col, k_col, v_col,
                   with_ctx, lam_init):
    t = p.shape[0]
    tpb = rows_per_batch // ROW_TILE
    nq = tpb if with_ctx else tpb - 1
    off = 0 if with_ctx else 1
    w2 = 2 * HEAD_DIM
    vec = pl.BlockSpec((1, HEAD_DIM), lambda b, h, i: (0, 0))
    return pl.pallas_call(
        functools.partial(_diff_kernel, n_ctx=n_ctx, ctx_tile_first=with_ctx, lam_init=lam_init),
        out_shape=jax.ShapeDtypeStruct((nb * nq * ROW_TILE, nh * DIFF_DV), BF16),
        grid=(nb, nh, nq),
        in_specs=[
            pl.BlockSpec((ROW_TILE, w2), lambda b, h, i: (b * tpb + i + off, q_col // w2 + h)),
            pl.BlockSpec((rows_per_batch, w2), lambda b, h, i: (b, k_col // w2 + h)),
            pl.BlockSpec((rows_per_batch, DIFF_DV), lambda b, h, i: (b, v_col // DIFF_DV + h)),
            pl.BlockSpec((ROW_TILE, HEAD_DIM), lambda b, h, i: (i + off, 0)),
            pl.BlockSpec((ROW_TILE, HEAD_DIM), lambda b, h, i: (i + off, 0)),
            pl.BlockSpec((rows_per_batch, HEAD_DIM), lambda b, h, i: (0, 0)),
            pl.BlockSpec((rows_per_batch, HEAD_DIM), lambda b, h, i: (0, 0)),
            vec, vec, vec, vec,
            pl.BlockSpec((1, DIFF_DV), lambda b, h, i: (0, 0)),
        ],
        out_specs=pl.BlockSpec((ROW_TILE, DIFF_DV), lambda b, h, i: (b * nq + i, h)),
        scratch_shapes=[pltpu.VMEM((rows_per_batch, w2), BF16)],
        compiler_params=_cparams(("arbitrary", "arbitrary", "arbitrary")),
        name="diff_attention",
    )(p, p, p, cos, sin, cos, sin, lq1.reshape(1, -1), lk1.reshape(1, -1), lq2.reshape(1, -1), lk2.reshape(1, -1),
      gn.reshape(1, DIFF_DV))


def _route_kernel(lg_ref, bias_ref, o_ref):
    x = lg_ref[...] + bias_ref[...]
    lane = lax.broadcasted_iota(jnp.int32, x.shape, 1).astype(F32)
    neg = -jnp.inf

    def first_max(vals, mask):
        v = jnp.where(mask, vals, neg)
        m = jnp.max(v, axis=-1, keepdims=True)
        idx = jnp.min(jnp.where(mask & (v == m), lane, float(LANES)), axis=-1, keepdims=True)
        return m, idx

    gmask = lane < N_GROUPS
    gm, gidx = first_max(x, gmask)
    g_w = 1.0 / jnp.sum(jnp.where(gmask, jnp.exp(x - gm), 0.0), axis=-1, keepdims=True)
    lo = N_GROUPS + EXPERTS_PER_GROUP * gidx
    emask = (lane >= lo) & (lane < lo + EXPERTS_PER_GROUP)
    m1, i1 = first_max(x, emask)
    m2, i2 = first_max(x, emask & (lane != i1))
    r = jnp.exp(m2 - m1)
    w1 = g_w / (1.0 + r)
    w2 = g_w * r / (1.0 + r)
    out = jnp.where(lane == 0, i1 - N_GROUPS, 0.0)
    out = jnp.where(lane == 1, i2 - N_GROUPS, out)
    out = jnp.where(lane == 2, w1, out)
    out = jnp.where(lane == 3, w2, out)
    o_ref[...] = out


def route(logits, bias):
    t = logits.shape[0]
    return pl.pallas_call(
        _route_kernel,
        out_shape=jax.ShapeDtypeStruct((t, LANES), F32),
        grid=(t // ROW_TILE,),
        in_specs=[pl.BlockSpec((ROW_TILE, LANES), lambda i: (i, 0)), pl.BlockSpec((1, LANES), lambda i: (0, 0))],
        out_specs=pl.BlockSpec((ROW_TILE, LANES), lambda i: (i, 0)),
        compiler_params=_cparams(("arbitrary",)),
        name="route",
    )(logits, bias)


def dispatch_plan(ids):
    t = ids.shape[0]
    tps = SEG_ROWS // SEG_TILE
    n_src = 2 * t + N_EXPERTS * SEG_TILE
    n_seg = (n_src // SEG_TILE + (tps - 1) * N_EXPERTS) // tps + 1 + (N_EXPERTS * SEG_TILE) // SEG_ROWS + 2
    e = ids.reshape(-1)
    onehot = (e[:, None] == jnp.arange(N_EXPERTS, dtype=jnp.int32)[None, :]).astype(jnp.int32)
    csum = jnp.cumsum(onehot, axis=0)
    counts = csum[-1]
    tiles = (counts + SEG_TILE - 1) // SEG_TILE
    segs = (tiles + tps - 1) // tps
    seg_end = jnp.cumsum(segs)
    seg_start = seg_end - segs
    row_start = (jnp.cumsum(tiles) - tiles) * SEG_TILE
    rank = csum - 1
    cpos = jnp.sum(onehot * (rank + row_start[None, :]), axis=1)
    src = jnp.zeros((n_src,), jnp.int32).at[cpos].set(jnp.arange(2 * t, dtype=jnp.int32) // 2)
    sidx = jnp.arange(n_seg, dtype=jnp.int32)
    seg_expert = jnp.minimum(jnp.sum((seg_end[None, :] <= sidx[:, None]).astype(jnp.int32), axis=1), N_EXPERTS - 1)
    sel = (seg_expert[:, None] == jnp.arange(N_EXPERTS, dtype=jnp.int32)[None, :]).astype(jnp.int32)
    k = sidx - jnp.sum(sel * seg_start[None, :], axis=1)
    seg_row0 = jnp.sum(sel * row_start[None, :], axis=1) + k * SEG_ROWS
    used = sidx < seg_end[-1]
    seg_nt = jnp.where(used, jnp.clip(jnp.sum(sel * tiles[None, :], axis=1) - k * tps, 0, tps), 0)
    y_rows = n_src + SEG_ROWS
    fill0 = jnp.max(jnp.where(used, seg_row0 + SEG_ROWS, 0)) + (sidx - seg_end[-1]) * SEG_ROWS
    seg_row0 = jnp.where(used, seg_row0, jnp.minimum(fill0, y_rows - SEG_ROWS))
    seg_fill = jnp.where(used, 0, (fill0 < y_rows).astype(jnp.int32))
    return src, cpos.reshape(t, 2), seg_expert, seg_row0, seg_nt, seg_fill, seg_end[-1:]


def _row_copy(src_hbm, idx, dst_vmem, r, sem):
    return pltpu.make_async_copy(src_hbm.at[pl.ds(idx, 1)], dst_vmem.at[pl.ds(r, 1)], sem)


def _expert_kernel(se_ref, r0_ref, nt_ref, fill_ref, nseg_ref, src_ref, h_hbm, wg_hbm, wu_hbm, wd_hbm, y_hbm,
                   land, xs, acc_a, acc_u, hid, st_g, st_u, st_d, wb_g, wb_u, wb_d, yst,
                   sem_x, sem_g, sem_u, sem_d, sem_y, *, expert_base):
    s = pl.program_id(0)
    n_seg = nseg_ref[0]
    d = xs.shape[1]
    ck = st_g.shape[1]
    cn = st_d.shape[2]
    n_k = d // ck
    n_c = d // cn
    assert n_k > W_SLOTS and n_c > W_SLOTS and n_c >= 3

    def start_gather(seg):
        r0 = r0_ref[seg]

        def body(r, carry):
            _row_copy(h_hbm, src_ref[r0 + r], land, r, sem_x).start()
            return carry

        lax.fori_loop(0, nt_ref[seg] * SEG_TILE, body, 0)

    def finish_gather(seg):
        nt = nt_ref[seg]

        def wait(t, carry):
            pltpu.make_async_copy(h_hbm.at[pl.ds(0, SEG_TILE)], land.at[pl.ds(0, SEG_TILE)], sem_x).wait()
            return carry

        lax.fori_loop(0, nt, wait, 0)

        def convert(t, carry):
            rows = pl.ds(pl.multiple_of(t * SEG_TILE, SEG_TILE), SEG_TILE)
            xs[rows, :] = land[rows, :].astype(BF16)
            return carry

        lax.fori_loop(0, nt, convert, 0)

    def in_copies(e, kc):
        slot = kc % W_SLOTS
        rows = pl.ds(pl.multiple_of(kc * ck, ck), ck)
        return (pltpu.make_async_copy(wg_hbm.at[e, rows], st_g.at[slot], sem_g.at[slot]),
                pltpu.make_async_copy(wu_hbm.at[e, rows], st_u.at[slot], sem_u.at[slot]))

    def out_copy(e, c):
        slot = c % W_SLOTS
        cols = pl.ds(pl.multiple_of(c * cn, cn), cn)
        return pltpu.make_async_copy(wd_hbm.at[e, :, cols], st_d.at[slot], sem_d.at[slot])

    def y_copy(c, slot=None):
        slot = c % 2 if slot is None else slot
        cols = pl.ds(pl.multiple_of(c * cn, cn), cn)
        rows = pl.ds(pl.multiple_of(r0_ref[s], SEG_TILE), SEG_ROWS)
        return pltpu.make_async_copy(yst.at[slot], y_hbm.at[rows, cols], sem_y.at[slot])

    def land_in(e, kc):
        for cp in in_copies(e, kc):
            cp.wait()
        wb_g[kc % 2] = st_g[kc % W_SLOTS].astype(BF16)
        wb_u[kc % 2] = st_u[kc % W_SLOTS].astype(BF16)

        @pl.when(kc + W_SLOTS < n_k)
        def _():
            for cp in in_copies(e, kc + W_SLOTS):
                cp.start()

    def land_out(e, c):
        out_copy(e, c).wait()
        wb_d[c % 2] = st_d[c % W_SLOTS].astype(BF16)

        @pl.when(c + W_SLOTS < n_c)
        def _():
            out_copy(e, c + W_SLOTS).start()

    def gate_up(kc):
        x_k = xs[:, pl.ds(pl.multiple_of(kc * ck, ck), ck)]
        acc_a[...] += _dot(x_k, wb_g[kc % 2])
        acc_u[...] += _dot(x_k, wb_u[kc % 2])

    def down(c):
        yst[c % 2] = _dot(hid[...], wb_d[c % 2])
        y_copy(c).start()

    @pl.when(s == 0)
    def _():
        xs[...] = jnp.zeros_like(xs)
        start_gather(0)

    @pl.when(s < n_seg)
    def _():
        e = expert_base + se_ref[s]
        for i in range(W_SLOTS):
            for cp in in_copies(e, i):
                cp.start()
            out_copy(e, i).start()
        finish_gather(s)

        @pl.when(s + 1 < n_seg)
        def _():
            start_gather(s + 1)

        acc_a[...] = jnp.zeros_like(acc_a)
        acc_u[...] = jnp.zeros_like(acc_u)
        land_in(e, 0)

        def in_step(kc, carry):
            land_in(e, kc + 1)
            gate_up(kc)
            return carry

        lax.fori_loop(0, n_k - 1, in_step, 0)
        land_out(e, 0)
        gate_up(n_k - 1)
        hid[...] = (_silu(acc_a[...]) * acc_u[...]).astype(BF16)

        def out_step(c, carry):
            @pl.when(c >= 2)
            def _():
                y_copy(c - 2).wait()

            land_out(e, c + 1)
            down(c)
            return carry

        lax.fori_loop(0, n_c - 1, out_step, 0)
        y_copy(n_c - 3).wait()
        down(n_c - 1)
        y_copy(n_c - 2).wait()
        y_copy(n_c - 1).wait()

    @pl.when(fill_ref[s] == 1)
    def _():
        yst[0] = jnp.zeros_like(yst[0])
        for c in range(n_c):
            y_copy(c, 0).start()
        for c in range(n_c):
            y_copy(c, 0).wait()


def expert_mlp(h, plan, wg, wu, wd, *, expert_base):
    src, _, seg_expert, seg_row0, seg_nt, seg_fill, n_seg = plan
    d = h.shape[1]
    f = wg.shape[2]
    n_grid = seg_expert.shape[0]
    ck = d // W_CHUNKS
    cn = d // W_CHUNKS
    return pl.pallas_call(
        functools.partial(_expert_kernel, expert_base=expert_base),
        out_shape=jax.ShapeDtypeStruct((src.shape[0] + SEG_ROWS, d), F32),
        grid_spec=pltpu.PrefetchScalarGridSpec(
            num_scalar_prefetch=6,
            grid=(n_grid,),
            in_specs=[pl.BlockSpec(memory_space=pl.ANY)] * 4,
            out_specs=pl.BlockSpec(memory_space=pl.ANY),
            scratch_shapes=[
                pltpu.VMEM((SEG_ROWS, d), F32), pltpu.VMEM((SEG_ROWS, d), BF16),
                pltpu.VMEM((SEG_ROWS, f), F32), pltpu.VMEM((SEG_ROWS, f), F32), pltpu.VMEM((SEG_ROWS, f), BF16),
                pltpu.VMEM((W_SLOTS, ck, f), F32), pltpu.VMEM((W_SLOTS, ck, f), F32), pltpu.VMEM((W_SLOTS, f, cn), F32),
                pltpu.VMEM((2, ck, f), BF16), pltpu.VMEM((2, ck, f), BF16), pltpu.VMEM((2, f, cn), BF16),
                pltpu.VMEM((2, SEG_ROWS, cn), F32),
                pltpu.SemaphoreType.DMA, pltpu.SemaphoreType.DMA((W_SLOTS,)), pltpu.SemaphoreType.DMA((W_SLOTS,)),
                pltpu.SemaphoreType.DMA((W_SLOTS,)), pltpu.SemaphoreType.DMA((2,)),
            ],
        ),
        compiler_params=_cparams(("arbitrary",)),
        name="moe_experts",
    )(seg_expert, seg_row0, seg_nt, seg_fill, n_seg, src, h, wg, wu, wd)


def _combine_kernel(p1_ref, p2_ref, x_ref, g_ref, r_ref, y_hbm, o_ref, buf_ref, sem):
    i = pl.program_id(0)
    rows = x_ref.shape[0]

    def start(r, carry):
        _row_copy(y_hbm, p1_ref[i * rows + r], buf_ref.at[0], r, sem).start()
        _row_copy(y_hbm, p2_ref[i * rows + r], buf_ref.at[1], r, sem).start()
        return carry

    lax.fori_loop(0, rows, start, 0)
    pltpu.make_async_copy(y_hbm.at[pl.ds(0, rows)], buf_ref.at[0], sem).wait()
    pltpu.make_async_copy(y_hbm.at[pl.ds(0, rows)], buf_ref.at[1], sem).wait()
    w1 = r_ref[:, 2:3]
    w2 = r_ref[:, 3:4]
    o_ref[...] = x_ref[...] + g_ref[0] * (w1 * buf_ref[0] + w2 * buf_ref[1])


def moe_combine(x, y, pos, routed, mod3, gate_chunk, mod_row):
    t, d = x.shape
    return pl.pallas_call(
        _combine_kernel,
        out_shape=jax.ShapeDtypeStruct((t, d), F32),
        grid_spec=pltpu.PrefetchScalarGridSpec(
            num_scalar_prefetch=2,
            grid=(t // ROW_TILE,),
            in_specs=[
                pl.BlockSpec((ROW_TILE, d), lambda i, a, b: (i, 0)),
                pl.BlockSpec((1, 1, d), lambda i, a, b: (mod_row(i), 0, gate_chunk)),
                pl.BlockSpec((ROW_TILE, LANES), lambda i, a, b: (i, 0)),
                pl.BlockSpec(memory_space=pl.ANY),
            ],
            out_specs=pl.BlockSpec((ROW_TILE, d), lambda i, a, b: (i, 0)),
            scratch_shapes=[pltpu.VMEM((2, ROW_TILE, d), F32), pltpu.SemaphoreType.DMA],
        ),
        compiler_params=_cparams(("arbitrary",)),
        name="moe_combine",
    )(pos[:, 0], pos[:, 1], x, mod3, routed, y)


def kernel(x, c, ctx, c_ctx, ada_w, ada_b, norm1_g, w_in, gla_wa_fwd, gla_ba_fwd, gla_wa_bwd, gla_ba_bwd, gla_norm_g,
           gqa_qnorm_g, gqa_knorm_g, diff_lq1, diff_lk1, diff_lq2, diff_lk2, diff_norm_g, w_out, norm2_g, router_wg,
           router_bg, router_we, router_be, moe_w_gate, moe_w_up, moe_w_down, final_norm_g):
    nb, seq, d = x.shape
    n_ctx = ctx.shape[1]
    depth = ada_w.shape[0]
    assert n_ctx == ROW_TILE and seq % ROW_TILE == 0 and seq % GRID_W == 0
    rpb = n_ctx + seq
    tpb = rpb // ROW_TILE
    t = nb * rpb

    gla_heads = (d // 4) // GLA_DV
    gqa_heads = (d // 2) // HEAD_DIM
    n_kv = gqa_heads // GQA_GROUP
    diff_heads = (d // 4) // DIFF_DV
    kw = gla_heads * GLA_DK
    vw = gla_heads * GLA_DV
    splits = (kw, kw, vw, vw, GLA_RANK, GLA_RANK, gqa_heads * HEAD_DIM, n_kv * HEAD_DIM, n_kv * HEAD_DIM,
              diff_heads * 2 * HEAD_DIM, diff_heads * 2 * HEAD_DIM, diff_heads * DIFF_DV)
    offs = [0]
    for s in splits:
        offs.append(offs[-1] + s)
    col = {}
    acc = 0
    for name, width in (("aq", kw), ("ak", kw), ("av", vw), ("ag", vw), ("bq", splits[6]), ("bk", splits[7]),
                        ("bv", splits[8]), ("dq", splits[9]), ("dk", splits[10]), ("dv", splits[11])):
        col[name] = acc
        acc += width

    def mod_row_full(i):
        return jnp.where(i % tpb == 0, nb, i // tpb)

    def mod_row_latent(i):
        return i // (tpb - 1)

    tokens = jnp.concatenate([ctx, x], axis=1).reshape(t, d)
    c_all = jnp.zeros((8, d), F32).at[:nb].set(c).at[nb].set(c_ctx)
    mod = ada_modulation(c_all, ada_w, ada_b)
    cos, sin = rope_tables(n_ctx, seq)
    w_wide = jnp.concatenate([w_in[:, :, :offs[4]], w_in[:, :, offs[6]:]], axis=2).astype(BF16)
    w_out_bf = w_out.astype(BF16)
    f = moe_w_gate.shape[-1]
    wg_all = moe_w_gate.reshape(depth * N_EXPERTS, d, f)
    wu_all = moe_w_up.reshape(depth * N_EXPERTS, d, f)
    wd_all = moe_w_down.reshape(depth * N_EXPERTS, f, d)

    for l in range(depth):
        last = l == depth - 1
        lam_init = 0.8 - 0.6 * math.exp(-0.3 * l)
        mod3 = mod[l].reshape(8, 1, 6 * d)

        w_dec = jnp.zeros((d, LANES), F32).at[:, :2 * GLA_RANK].set(w_in[l, :, offs[4]:offs[6]])
        h, paa = norm_modulate(tokens, norm1_g[l], mod3, 1, 0, w_dec, mod_row_full, h_dtype=BF16, exact_small=False)
        p = matmul(h, w_wide, l, bm=1024, bn=1024, out_dtype=BF16)

        wa_blk = jnp.zeros((2, LANES, kw), F32)
        wa_blk = wa_blk.at[0, :GLA_RANK].set(gla_wa_fwd[l]).at[1, GLA_RANK:2 * GLA_RANK].set(gla_wa_bwd[l])
        ba_blk = jnp.stack([gla_ba_fwd[l], gla_ba_bwd[l]]).reshape(2, 1, kw)
        o_f, o_b = gla_scan(p, paa, wa_blk, ba_blk, nb=nb, rows_per_batch=rpb, n_ctx_chunks=n_ctx // GLA_CHUNK,
                            nh=gla_heads)
        if last:
            x_tile, mod_row = _LatentTiles(nb, tpb), mod_row_latent
        else:
            x_tile, mod_row = _AllTiles(t // ROW_TILE), mod_row_full
        mix_a = gla_output(o_f, o_b, p, gla_norm_g[l], x_tile, gate_block=col["ag"] // vw, nh=gla_heads)
        mix_b = gqa_attention(p, cos, sin, gqa_qnorm_g[l], gqa_knorm_g[l], nb=nb, rows_per_batch=rpb, n_ctx=n_ctx,
                              n_kv=n_kv, q_col=col["bq"], k_col=col["bk"], v_col=col["bv"], with_ctx=not last)
        mix_c = diff_attention(p, cos, sin, diff_lq1[l], diff_lk1[l], diff_lq2[l], diff_lk2[l], diff_norm_g[l],
                               nb=nb, rows_per_batch=rpb, n_ctx=n_ctx, nh=diff_heads, q_col=col["dq"],
                               k_col=col["dk"], v_col=col["dv"], with_ctx=not last, lam_init=lam_init)
        tokens = matmul3_gated_residual(mix_a, mix_b, mix_c, w_out_bf, l, tokens, mod3, 2, x_tile, mod_row, bn=1024)

        w_route = jnp.zeros((d, LANES), F32).at[:, :N_GROUPS].set(router_wg[l])
        w_route = w_route.at[:, N_GROUPS:N_GROUPS + N_EXPERTS].set(router_we[l])
        b_route = jnp.zeros((1, LANES), F32).at[0, :N_GROUPS].set(router_bg[l])
        b_route = b_route.at[0, N_GROUPS:N_GROUPS + N_EXPERTS].set(router_be[l])
        h2, logits = norm_modulate(tokens, norm2_g[l], mod3, 4, 3, w_route, mod_row, h_dtype=F32, exact_small=True)
        routed = route(logits, b_route)
        plan = dispatch_plan(routed[:, 0:2].astype(jnp.int32))
        y = expert_mlp(h2, plan, wg_all, wu_all, wd_all, expert_base=l * N_EXPERTS)
        tokens = moe_combine(tokens, y, plan[1], routed, mod3, 5, mod_row)

    return final_norm(tokens, final_norm_g).reshape(nb, seq, d)
```

```python
import functools
import math

import jax
import jax.numpy as jnp
from jax import lax
from jax.experimental import pallas as pl
from jax.experimental.pallas import tpu as pltpu

F32 = jnp.float32
BF16 = jnp.bfloat16

HEAD_DIM = 128
GRID_W = 64
ROPE_THETA = 10000.0
EPS = 1e-6
GLA_DK = HEAD_DIM
GLA_DV = 2 * HEAD_DIM
GLA_RANK = 16
GLA_TAU = 16.0
GLA_CHUNK = 64
GQA_GROUP = 4
GQA_KV_PER_STEP = 2
DIFF_DV = 2 * HEAD_DIM
N_GROUPS = 4
EXPERTS_PER_GROUP = 8
N_EXPERTS = N_GROUPS * EXPERTS_PER_GROUP

LANES = 128
ROW_TILE = 256
SEG_TILE = 128
SEG_ROWS = 768
W_CHUNKS = 8
W_SLOTS = 3
VMEM_LIMIT = 56 * 1024 * 1024


def _cparams(sem, vmem=VMEM_LIMIT):
    return pltpu.CompilerParams(dimension_semantics=sem, vmem_limit_bytes=vmem)


def _sigmoid(x):
    return 1.0 / (1.0 + jnp.exp(-x))


def _silu(x):
    return x * _sigmoid(x)


def _log_sigmoid(x):
    return jnp.minimum(x, 0.0) - jnp.log(1.0 + jnp.exp(-jnp.abs(x)))


def _rms(x, g):
    return x * lax.rsqrt(jnp.mean(x * x, axis=-1, keepdims=True) + EPS) * g


def _dot(a, b):
    return jnp.dot(a, b, preferred_element_type=F32)


def _dot_nt(a, b):
    return lax.dot_general(a, b, (((1,), (1,)), ((), ())), preferred_element_type=F32)


def _dot_tn(a, b):
    return lax.dot_general(a, b, (((0,), (0,)), ((), ())), preferred_element_type=F32)


class _AllTiles:
    def __init__(self, n):
        self.n = n

    def __call__(self, i):
        return i


class _LatentTiles:
    def __init__(self, nb, tiles_per_batch):
        self.n = nb * (tiles_per_batch - 1)
        self._lat = tiles_per_batch - 1
        self._tpb = tiles_per_batch

    def __call__(self, i):
        return (i // self._lat) * self._tpb + 1 + i % self._lat


def _ada_kernel(c_ref, w_ref, b_ref, o_ref):
    s = _silu(c_ref[...]).astype(BF16)
    o_ref[0] = _dot(s, w_ref[0].astype(BF16)) + b_ref[0]


def ada_modulation(c_all, ada_w, ada_b, *, tn=512):
    nl, d, n = ada_w.shape
    return pl.pallas_call(
        _ada_kernel,
        out_shape=jax.ShapeDtypeStruct((nl, 8, n), F32),
        grid=(nl, n // tn),
        in_specs=[
            pl.BlockSpec((8, d), lambda l, j: (0, 0)),
            pl.BlockSpec((1, d, tn), lambda l, j: (l, 0, j)),
            pl.BlockSpec((1, 1, tn), lambda l, j: (l, 0, j)),
        ],
        out_specs=pl.BlockSpec((1, 8, tn), lambda l, j: (l, 0, j)),
        compiler_params=_cparams(("arbitrary", "arbitrary")),
        name="ada_modulation",
    )(c_all, ada_w, ada_b.reshape(nl, 1, n))


def _norm_kernel(x_ref, g_ref, sc_ref, sh_ref, w_ref, h_ref, s_ref, *, exact_small):
    h = _rms(x_ref[...], g_ref[...]) * (1.0 + sc_ref[0]) + sh_ref[0]
    h_ref[...] = h.astype(h_ref.dtype)
    if exact_small:
        s_ref[...] = jnp.dot(h, w_ref[...], precision=lax.Precision.HIGHEST, preferred_element_type=F32)
    else:
        s_ref[...] = _dot(h.astype(BF16), w_ref[...].astype(BF16))


def norm_modulate(x, g, mod3, sc_chunk, sh_chunk, w_small, mod_row, *, h_dtype, exact_small):
    t, d = x.shape
    ns = w_small.shape[1]
    return pl.pallas_call(
        functools.partial(_norm_kernel, exact_small=exact_small),
        out_shape=(jax.ShapeDtypeStruct((t, d), h_dtype), jax.ShapeDtypeStruct((t, ns), F32)),
        grid=(t // ROW_TILE,),
        in_specs=[
            pl.BlockSpec((ROW_TILE, d), lambda i: (i, 0)),
            pl.BlockSpec((1, d), lambda i: (0, 0)),
            pl.BlockSpec((1, 1, d), lambda i: (mod_row(i), 0, sc_chunk)),
            pl.BlockSpec((1, 1, d), lambda i: (mod_row(i), 0, sh_chunk)),
            pl.BlockSpec((d, ns), lambda i: (0, 0)),
        ],
        out_specs=(pl.BlockSpec((ROW_TILE, d), lambda i: (i, 0)), pl.BlockSpec((ROW_TILE, ns), lambda i: (i, 0))),
        compiler_params=_cparams(("arbitrary",)),
        name="norm_modulate",
    )(x, g.reshape(1, d), mod3, mod3, w_small)


def _drop_cols_kernel(w_ref, o_ref, *, lo, hi):
    n = w_ref.shape[2]
    o_ref[0, :, 0:lo] = w_ref[0, :, 0:lo].astype(o_ref.dtype)
    o_ref[0, :, lo:lo + n - hi] = w_ref[0, :, hi:n].astype(o_ref.dtype)


def drop_cols_bf16(w, lo, hi, *, br=256):
    nl, k, n = w.shape
    n_out = n - (hi - lo)
    return pl.pallas_call(
        functools.partial(_drop_cols_kernel, lo=lo, hi=hi),
        out_shape=jax.ShapeDtypeStruct((nl, k, n_out), BF16),
        grid=(nl, k // br),
        in_specs=[pl.BlockSpec((1, br, n), lambda l, i: (l, i, 0))],
        out_specs=pl.BlockSpec((1, br, n_out), lambda l, i: (l, i, 0)),
        compiler_params=_cparams(("arbitrary", "arbitrary")),
        name="drop_cols_bf16",
    )(w)


def _mm_kernel(a_ref, b_ref, o_ref):
    o_ref[...] = _dot(a_ref[...], b_ref[...]).astype(o_ref.dtype)


def matmul(a, b_all, layer, *, bm, bn, out_dtype):
    m, k = a.shape
    n = b_all.shape[2]
    return pl.pallas_call(
        _mm_kernel,
        out_shape=jax.ShapeDtypeStruct((m, n), out_dtype),
        grid=(m // bm, n // bn),
        in_specs=[pl.BlockSpec((bm, k), lambda i, j: (i, 0)), pl.BlockSpec((None, k, bn), lambda i, j: (layer, 0, j))],
        out_specs=pl.BlockSpec((bm, bn), lambda i, j: (i, j)),
        compiler_params=_cparams(("arbitrary", "arbitrary")),
        name="matmul",
    )(a, b_all)


def _mm3_res_kernel(a1_ref, a2_ref, a3_ref, b_ref, x_ref, g_ref, o_ref):
    k1 = a1_ref.shape[1]
    k2 = k1 + a2_ref.shape[1]
    acc = _dot(a1_ref[...], b_ref[0:k1, :]) + _dot(a2_ref[...], b_ref[k1:k2, :]) + _dot(a3_ref[...], b_ref[k2:, :])
    o_ref[...] = x_ref[...] + g_ref[0] * acc


def matmul3_gated_residual(a1, a2, a3, b_all, layer, x, mod3, gate_chunk, x_tile, mod_row, *, bn):
    k = b_all.shape[1]
    n = b_all.shape[2]
    nbn = n // bn
    return pl.pallas_call(
        _mm3_res_kernel,
        out_shape=jax.ShapeDtypeStruct((x_tile.n * ROW_TILE, n), F32),
        grid=(nbn, x_tile.n),
        in_specs=[
            pl.BlockSpec((ROW_TILE, a1.shape[1]), lambda j, i: (i, 0)),
            pl.BlockSpec((ROW_TILE, a2.shape[1]), lambda j, i: (i, 0)),
            pl.BlockSpec((ROW_TILE, a3.shape[1]), lambda j, i: (i, 0)),
            pl.BlockSpec((None, k, bn), lambda j, i: (layer, 0, j)),
            pl.BlockSpec((ROW_TILE, bn), lambda j, i: (x_tile(i), j)),
            pl.BlockSpec((1, 1, bn), lambda j, i: (mod_row(i), 0, gate_chunk * nbn + j)),
        ],
        out_specs=pl.BlockSpec((ROW_TILE, bn), lambda j, i: (i, j)),
        compiler_params=_cparams(("arbitrary", "arbitrary")),
        name="matmul3_gated_residual",
    )(a1, a2, a3, b_all, x, mod3)


def _gla_direction(q_ref, k_ref, v_ref, paa_ref, wa, ba, o_ref, st_ref, *, nh, backward):
    c = GLA_CHUNK
    row = lax.broadcasted_iota(jnp.int32, (c, c), 0)
    col = lax.broadcasted_iota(jnp.int32, (c, c), 1)
    incl = (col >= row) if backward else (col <= row)
    tri = incl.astype(BF16)

    z = _dot(paa_ref[...].astype(BF16), wa.astype(BF16)) + ba
    la = _log_sigmoid(z) * (1.0 / GLA_TAU)
    la_hi = la.astype(BF16)
    la_lo = (la - la_hi.astype(F32)).astype(BF16)
    b = _dot(tri, la_hi) + _dot(tri, la_lo)
    tot = jnp.sum(la, axis=0, keepdims=True)
    mid = 0.5 * tot
    e_q = jnp.exp(b - mid)
    e_k = jnp.exp(mid - b)
    e_in = jnp.exp(b)
    e_out = jnp.exp(tot - b)
    e_tot = jnp.exp(tot)

    for h in range(nh):
        ks = slice(h * GLA_DK, (h + 1) * GLA_DK)
        vs = slice(h * GLA_DV, (h + 1) * GLA_DV)
        q = q_ref[:, ks].astype(F32) * (GLA_DK ** -0.5)
        k = k_ref[:, ks].astype(F32)
        v = v_ref[:, vs]
        s = _dot_nt((q * e_q[:, ks]).astype(BF16), (k * e_k[:, ks]).astype(BF16))
        s = jnp.where(incl, s, 0.0)
        st = st_ref[h]
        o = _dot(s.astype(BF16), v) + _dot_nt((q * e_in[:, ks]).astype(BF16), st.astype(BF16))
        st_ref[h] = st * e_tot[:, ks] + _dot_tn(v, (k * e_out[:, ks]).astype(BF16))
        o_ref[:, vs] = o


def _gla_kernel(qf, kf, vf, pf, qb, kb, vb, pb, wa_ref, ba_ref, of_ref, ob_ref, st_ref, *, nh):
    @pl.when(pl.program_id(1) == 0)
    def _():
        st_ref[...] = jnp.zeros_like(st_ref)

    _gla_direction(qf, kf, vf, pf, wa_ref[0], ba_ref[0], of_ref, st_ref.at[0], nh=nh, backward=False)
    _gla_direction(qb, kb, vb, pb, wa_ref[1], ba_ref[1], ob_ref, st_ref.at[1], nh=nh, backward=True)


def gla_scan(p, paa, wa_blk, ba_blk, *, nb, rows_per_batch, n_ctx_chunks, nh):
    t = p.shape[0]
    c = GLA_CHUNK
    ncb = rows_per_batch // c
    kw = nh * GLA_DK
    vw = nh * GLA_DV

    def fwd(b, g):
        return b * ncb + g

    def bwd(b, g):
        return b * ncb + jnp.where(g < n_ctx_chunks, n_ctx_chunks - 1 - g, ncb - 1 + n_ctx_chunks - g)

    def chunk_specs(cm):
        return [
            pl.BlockSpec((c, kw), lambda b, g: (cm(b, g), 0)),
            pl.BlockSpec((c, kw), lambda b, g: (cm(b, g), 1)),
            pl.BlockSpec((c, vw), lambda b, g: (cm(b, g), 2 * kw // vw)),
            pl.BlockSpec((c, LANES), lambda b, g: (cm(b, g), 0)),
        ]

    out = jax.ShapeDtypeStruct((t, vw), F32)
    return pl.pallas_call(
        functools.partial(_gla_kernel, nh=nh),
        out_shape=(out, out),
        grid=(nb, ncb),
        in_specs=chunk_specs(fwd) + chunk_specs(bwd) + [
            pl.BlockSpec((2, LANES, kw), lambda b, g: (0, 0, 0)),
            pl.BlockSpec((2, 1, kw), lambda b, g: (0, 0, 0)),
        ],
        out_specs=(pl.BlockSpec((c, vw), lambda b, g: (fwd(b, g), 0)), pl.BlockSpec((c, vw), lambda b, g: (bwd(b, g), 0))),
        scratch_shapes=[pltpu.VMEM((2, nh, GLA_DV, GLA_DK), F32)],
        compiler_params=_cparams(("arbitrary", "arbitrary")),
        name="gla_scan",
    )(p, p, p, paa, p, p, p, paa, wa_blk, ba_blk)


def _gla_out_kernel(of_ref, ob_ref, gate_ref, gn_ref, out_ref, *, nh):
    o = of_ref[...] + ob_ref[...]
    for h in range(nh):
        vs = slice(h * GLA_DV, (h + 1) * GLA_DV)
        gate = gate_ref[:, vs].astype(F32)
        out_ref[:, vs] = (_rms(o[:, vs], gn_ref[...]) * _silu(gate)).astype(out_ref.dtype)


def gla_output(o_f, o_b, p, gn, in_tile, *, gate_block, nh):
    vw = nh * GLA_DV
    row = pl.BlockSpec((ROW_TILE, vw), lambda i: (in_tile(i), 0))
    return pl.pallas_call(
        functools.partial(_gla_out_kernel, nh=nh),
        out_shape=jax.ShapeDtypeStruct((in_tile.n * ROW_TILE, vw), BF16),
        grid=(in_tile.n,),
        in_specs=[row, row, pl.BlockSpec((ROW_TILE, vw), lambda i: (in_tile(i), gate_block)),
                  pl.BlockSpec((1, GLA_DV), lambda i: (0, 0))],
        out_specs=pl.BlockSpec((ROW_TILE, vw), lambda i: (i, 0)),
        compiler_params=_cparams(("arbitrary",)),
        name="gla_output",
    )(o_f, o_b, p, gn.reshape(1, GLA_DV))


def _rope(x, cos, sin_signed):
    lane = lax.broadcasted_iota(jnp.int32, x.shape, 1)
    first = (lane % (HEAD_DIM // 2)) < (HEAD_DIM // 4)
    rot = jnp.where(first, pltpu.roll(x, HEAD_DIM - HEAD_DIM // 4, 1), pltpu.roll(x, HEAD_DIM // 4, 1))
    return x * cos + rot * sin_signed


def rope_tables(n_ctx, seq):
    nf = HEAD_DIM // 4
    rows = seq // GRID_W
    row = jnp.repeat(jnp.arange(rows, dtype=jnp.int32), GRID_W).astype(F32)
    col = jnp.tile(jnp.arange(GRID_W, dtype=jnp.int32), rows).astype(F32)
    inv = ROPE_THETA ** (-jnp.arange(nf, dtype=F32) / nf)
    ang = jnp.concatenate([row[:, None] * inv, row[:, None] * inv, col[:, None] * inv, col[:, None] * inv], axis=1)
    sign = jnp.tile(jnp.concatenate([-jnp.ones((nf,), F32), jnp.ones((nf,), F32)]), 2)
    cos = jnp.concatenate([jnp.ones((n_ctx, HEAD_DIM), F32), jnp.cos(ang)], axis=0)
    sin = jnp.concatenate([jnp.zeros((n_ctx, HEAD_DIM), F32), jnp.sin(ang) * sign], axis=0)
    return cos, sin


def _attend(q, keys, vals):
    s = _dot_nt(q, keys)
    e = jnp.exp(s - jnp.max(s, axis=-1, keepdims=True))
    return _dot(e.astype(BF16), vals), jnp.sum(e, axis=-1, keepdims=True)


def _gqa_kernel(q_ref, k_ref, v_ref, cq_ref, sq_ref, ck_ref, sk_ref, qn_ref, kn_ref, o_ref, ks_ref, *,
                n_ctx, ctx_tile_first):
    qi = pl.program_id(2)

    n_kv = k_ref.shape[1] // HEAD_DIM

    @pl.when(qi == 0)
    def _():
        for j in range(n_kv):
            cs = slice(j * HEAD_DIM, (j + 1) * HEAD_DIM)
            k = _rms(k_ref[:, cs].astype(F32), kn_ref[...])
            ks_ref[:, cs] = _rope(k, ck_ref[...], sk_ref[...]).astype(BF16)

    def attend(nk):
        for j in range(n_kv):
            cs = slice(j * HEAD_DIM, (j + 1) * HEAD_DIM)
            vals = v_ref[0:nk, cs]
            for g in range(GQA_GROUP):
                head = j * GQA_GROUP + g
                hs = slice(head * HEAD_DIM, (head + 1) * HEAD_DIM)
                q = _rms(q_ref[:, hs].astype(F32), qn_ref[...])
                q = (_rope(q, cq_ref[...], sq_ref[...]) * (HEAD_DIM ** -0.5)).astype(BF16)
                pv, l = _attend(q, ks_ref[0:nk, cs], vals)
                o_ref[:, hs] = (pv / l).astype(o_ref.dtype)

    if ctx_tile_first:
        @pl.when(qi == 0)
        def _():
            attend(n_ctx)

        @pl.when(qi > 0)
        def _():
            attend(k_ref.shape[0])
    else:
        attend(k_ref.shape[0])


def gqa_attention(p, cos, sin, qn, kn, *, nb, rows_per_batch, n_ctx, n_kv, q_col, k_col, v_col, with_ctx):
    t = p.shape[0]
    tpb = rows_per_batch // ROW_TILE
    nq = tpb if with_ctx else tpb - 1
    off = 0 if with_ctx else 1
    kvw = GQA_KV_PER_STEP * HEAD_DIM
    qw = GQA_GROUP * kvw
    return pl.pallas_call(
        functools.partial(_gqa_kernel, n_ctx=n_ctx, ctx_tile_first=with_ctx),
        out_shape=jax.ShapeDtypeStruct((nb * nq * ROW_TILE, n_kv * GQA_GROUP * HEAD_DIM), BF16),
        grid=(nb, n_kv // GQA_KV_PER_STEP, nq),
        in_specs=[
            pl.BlockSpec((ROW_TILE, qw), lambda b, h, i: (b * tpb + i + off, q_col // qw + h)),
            pl.BlockSpec((rows_per_batch, kvw), lambda b, h, i: (b, k_col // kvw + h)),
            pl.BlockSpec((rows_per_batch, kvw), lambda b, h, i: (b, v_col // kvw + h)),
            pl.BlockSpec((ROW_TILE, HEAD_DIM), lambda b, h, i: (i + off, 0)),
            pl.BlockSpec((ROW_TILE, HEAD_DIM), lambda b, h, i: (i + off, 0)),
            pl.BlockSpec((rows_per_batch, HEAD_DIM), lambda b, h, i: (0, 0)),
            pl.BlockSpec((rows_per_batch, HEAD_DIM), lambda b, h, i: (0, 0)),
            pl.BlockSpec((1, HEAD_DIM), lambda b, h, i: (0, 0)),
            pl.BlockSpec((1, HEAD_DIM), lambda b, h, i: (0, 0)),
        ],
        out_specs=pl.BlockSpec((ROW_TILE, qw), lambda b, h, i: (b * nq + i, h)),
        scratch_shapes=[pltpu.VMEM((rows_per_batch, kvw), BF16)],
        compiler_params=_cparams(("arbitrary", "arbitrary", "arbitrary")),
        name="gqa_attention",
    )(p, p, p, cos, sin, cos, sin, qn.reshape(1, HEAD_DIM), kn.reshape(1, HEAD_DIM))


def _diff_kernel(q_ref, k_ref, v_ref, cq_ref, sq_ref, ck_ref, sk_ref, lq1_ref, lk1_ref, lq2_ref, lk2_ref, gn_ref,
                 o_ref, ks_ref, *, n_ctx, ctx_tile_first, lam_init):
    qi = pl.program_id(1)
    nh = v_ref.shape[1] // DIFF_DV

    @pl.when(qi == 0)
    def _():
        for j in range(2 * nh):
            hs = slice(j * HEAD_DIM, (j + 1) * HEAD_DIM)
            ks_ref[:, hs] = _rope(k_ref[:, hs].astype(F32), ck_ref[...], sk_ref[...]).astype(BF16)

    lam = (jnp.exp(jnp.sum(lq1_ref[...] * lk1_ref[...], axis=-1, keepdims=True))
           - jnp.exp(jnp.sum(lq2_ref[...] * lk2_ref[...], axis=-1, keepdims=True)) + lam_init)

    def attend(nk):
        for h in range(nh):
            vs = slice(h * DIFF_DV, (h + 1) * DIFF_DV)
            vals = v_ref[0:nk, vs]
            o = None
            for j in range(2):
                hs = slice((2 * h + j) * HEAD_DIM, (2 * h + j + 1) * HEAD_DIM)
                q = (_rope(q_ref[:, hs].astype(F32), cq_ref[...], sq_ref[...]) * (HEAD_DIM ** -0.5)).astype(BF16)
                pv, l = _attend(q, ks_ref[0:nk, hs], vals)
                o = pv / l if j == 0 else o - (lam / l) * pv
            o_ref[:, vs] = (_rms(o, gn_ref[...]) * (1.0 - lam_init)).astype(o_ref.dtype)

    if ctx_tile_first:
        @pl.when(qi == 0)
        def _():
            attend(n_ctx)

        @pl.when(qi > 0)
        def _():
            attend(k_ref.shape[0])
    else:
        attend(k_ref.shape[0])


def diff_attention(p, cos, sin, lq1, lk1, lq2, lk2, gn, *, nb, rows_per_batch, n_ctx, nh, q_col, k_col, v_col,
                   with_ctx, lam_init):
    t = p.shape[0]
    tpb = rows_per_batch // ROW_TILE
    nq = tpb if with_ctx else tpb - 1
    off = 0 if with_ctx else 1
    w = nh * DIFF_DV
    vec = pl.BlockSpec((1, HEAD_DIM), lambda b, i: (0, 0))
    return pl.pallas_call(
        functools.partial(_diff_kernel, n_ctx=n_ctx, ctx_tile_first=with_ctx, lam_init=lam_init),
        out_shape=jax.ShapeDtypeStruct((nb * nq * ROW_TILE, w), BF16),
        grid=(nb, nq),
        in_specs=[
            pl.BlockSpec((ROW_TILE, w), lambda b, i: (b * tpb + i + off, q_col // w)),
            pl.BlockSpec((rows_per_batch, w), lambda b, i: (b, k_col // w)),
            pl.BlockSpec((rows_per_batch, w), lambda b, i: (b, v_col // w)),
            pl.BlockSpec((ROW_TILE, HEAD_DIM), lambda b, i: (i + off, 0)),
            pl.BlockSpec((ROW_TILE, HEAD_DIM), lambda b, i: (i + off, 0)),
            pl.BlockSpec((rows_per_batch, HEAD_DIM), lambda b, i: (0, 0)),
            pl.BlockSpec((rows_per_batch, HEAD_DIM), lambda b, i: (0, 0)),
            vec, vec, vec, vec,
            pl.BlockSpec((1, DIFF_DV), lambda b, i: (0, 0)),
        ],
        out_specs=pl.BlockSpec((ROW_TILE, w), lambda b, i: (b * nq + i, 0)),
        scratch_shapes=[pltpu.VMEM((rows_per_batch, w), BF16)],
        compiler_params=_cparams(("arbitrary", "arbitrary")),
        name="diff_attention",
    )(p, p, p, cos, sin, cos, sin, lq1.reshape(1, -1), lk1.reshape(1, -1), lq2.reshape(1, -1), lk2.reshape(1, -1),
      gn.reshape(1, DIFF_DV))


def _route_kernel(lg_ref, bias_ref, o_ref):
    x = lg_ref[...] + bias_ref[...]
    lane = lax.broadcasted_iota(jnp.int32, x.shape, 1).astype(F32)
    neg = -jnp.inf

    def first_max(vals, mask):
        v = jnp.where(mask, vals, neg)
        m = jnp.max(v, axis=-1, keepdims=True)
        idx = jnp.min(jnp.where(mask & (v == m), lane, float(LANES)), axis=-1, keepdims=True)
        return m, idx

    gmask = lane < N_GROUPS
    gm, gidx = first_max(x, gmask)
    g_w = 1.0 / jnp.sum(jnp.where(gmask, jnp.exp(x - gm), 0.0), axis=-1, keepdims=True)
    lo = N_GROUPS + EXPERTS_PER_GROUP * gidx
    emask = (lane >= lo) & (lane < lo + EXPERTS_PER_GROUP)
    m1, i1 = first_max(x, emask)
    m2, i2 = first_max(x, emask & (lane != i1))
    r = jnp.exp(m2 - m1)
    w1 = g_w / (1.0 + r)
    w2 = g_w * r / (1.0 + r)
    out = jnp.where(lane == 0, i1 - N_GROUPS, 0.0)
    out = jnp.where(lane == 1, i2 - N_GROUPS, out)
    out = jnp.where(lane == 2, w1, out)
    out = jnp.where(lane == 3, w2, out)
    o_ref[...] = out


def route(logits, bias):
    t = logits.shape[0]
    return pl.pallas_call(
        _route_kernel,
        out_shape=jax.ShapeDtypeStruct((t, LANES), F32),
        grid=(t // ROW_TILE,),
        in_specs=[pl.BlockSpec((ROW_TILE, LANES), lambda i: (i, 0)), pl.BlockSpec((1, LANES), lambda i: (0, 0))],
        out_specs=pl.BlockSpec((ROW_TILE, LANES), lambda i: (i, 0)),
        compiler_params=_cparams(("arbitrary",)),
        name="route",
    )(logits, bias)


def dispatch_plan(ids):
    t = ids.shape[0]
    tps = SEG_ROWS // SEG_TILE
    n_src = 2 * t + N_EXPERTS * SEG_TILE
    n_seg = (n_src // SEG_TILE + (tps - 1) * N_EXPERTS) // tps + 1 + (N_EXPERTS * SEG_TILE) // SEG_ROWS + 2
    e = ids.reshape(-1)
    onehot = (e[:, None] == jnp.arange(N_EXPERTS, dtype=jnp.int32)[None, :]).astype(jnp.int32)
    csum = jnp.cumsum(onehot, axis=0)
    counts = csum[-1]
    tiles = (counts + SEG_TILE - 1) // SEG_TILE
    segs = (tiles + tps - 1) // tps
    seg_end = jnp.cumsum(segs)
    seg_start = seg_end - segs
    row_start = (jnp.cumsum(tiles) - tiles) * SEG_TILE
    rank = csum - 1
    cpos = jnp.sum(onehot * (rank + row_start[None, :]), axis=1)
    src = jnp.zeros((n_src,), jnp.int32).at[cpos].set(jnp.arange(2 * t, dtype=jnp.int32) // 2)
    sidx = jnp.arange(n_seg, dtype=jnp.int32)
    seg_expert = jnp.minimum(jnp.sum((seg_end[None, :] <= sidx[:, None]).astype(jnp.int32), axis=1), N_EXPERTS - 1)
    sel = (seg_expert[:, None] == jnp.arange(N_EXPERTS, dtype=jnp.int32)[None, :]).astype(jnp.int32)
    k = sidx - jnp.sum(sel * seg_start[None, :], axis=1)
    seg_row0 = jnp.sum(sel * row_start[None, :], axis=1) + k * SEG_ROWS
    used = sidx < seg_end[-1]
    seg_nt = jnp.where(used, jnp.clip(jnp.sum(sel * tiles[None, :], axis=1) - k * tps, 0, tps), 0)
    y_rows = n_src + SEG_ROWS
    fill0 = jnp.max(jnp.where(used, seg_row0 + SEG_ROWS, 0)) + (sidx - seg_end[-1]) * SEG_ROWS
    seg_row0 = jnp.where(used, seg_row0, jnp.minimum(fill0, y_rows - SEG_ROWS))
    seg_fill = jnp.where(used, 0, (fill0 < y_rows).astype(jnp.int32))
    return src, cpos.reshape(t, 2), seg_expert, seg_row0, seg_nt, seg_fill, seg_end[-1:]


def _row_copy(src_hbm, idx, dst_vmem, r, sem):
    return pltpu.make_async_copy(src_hbm.at[pl.ds(idx, 1)], dst_vmem.at[pl.ds(r, 1)], sem)


def _expert_kernel(se_ref, r0_ref, nt_ref, fill_ref, nseg_ref, src_ref, h_hbm, wg_hbm, wu_hbm, wd_hbm, y_hbm,
                   land, xs, acc_a, acc_u, hid, st_g, st_u, st_d, wb_g, wb_u, wb_d, yst,
                   sem_x, sem_g, sem_u, sem_d, sem_y, *, expert_base):
    s = pl.program_id(0)
    n_seg = nseg_ref[0]
    d = xs.shape[1]
    ck = st_g.shape[1]
    cn = st_d.shape[2]
    n_k = d // ck
    n_c = d // cn
    assert n_k > W_SLOTS and n_c > W_SLOTS and n_c >= 3

    def start_gather(seg):
        r0 = r0_ref[seg]

        def tile(tt, carry):
            base = tt * SEG_TILE

            def body(r, c):
                _row_copy(h_hbm, src_ref[r0 + base + r], land, base + r, sem_x).start()
                return c

            return lax.fori_loop(0, SEG_TILE, body, carry, unroll=8)

        lax.fori_loop(0, nt_ref[seg], tile, 0)

    def finish_gather(seg):
        nt = nt_ref[seg]

        def wait(t, carry):
            pltpu.make_async_copy(h_hbm.at[pl.ds(0, SEG_TILE)], land.at[pl.ds(0, SEG_TILE)], sem_x).wait()
            return carry

        lax.fori_loop(0, nt, wait, 0)

        def convert(t, carry):
            rows = pl.ds(pl.multiple_of(t * SEG_TILE, SEG_TILE), SEG_TILE)
            xs[rows, :] = land[rows, :].astype(BF16)
            return carry

        lax.fori_loop(0, nt, convert, 0)

    def in_copies(e, kc):
        slot = kc % W_SLOTS
        rows = pl.ds(pl.multiple_of(kc * ck, ck), ck)
        return (pltpu.make_async_copy(wg_hbm.at[e, rows], st_g.at[slot], sem_g.at[slot]),
                pltpu.make_async_copy(wu_hbm.at[e, rows], st_u.at[slot], sem_u.at[slot]))

    def out_copy(e, c):
        slot = c % W_SLOTS
        cols = pl.ds(pl.multiple_of(c * cn, cn), cn)
        return pltpu.make_async_copy(wd_hbm.at[e, :, cols], st_d.at[slot], sem_d.at[slot])

    def y_copy(c, slot=None):
        slot = c % 2 if slot is None else slot
        cols = pl.ds(pl.multiple_of(c * cn, cn), cn)
        rows = pl.ds(pl.multiple_of(r0_ref[s], SEG_TILE), SEG_ROWS)
        return pltpu.make_async_copy(yst.at[slot], y_hbm.at[rows, cols], sem_y.at[slot])

    def land_in(e, kc):
        for cp in in_copies(e, kc):
            cp.wait()
        wb_g[kc % 2] = st_g[kc % W_SLOTS].astype(BF16)
        wb_u[kc % 2] = st_u[kc % W_SLOTS].astype(BF16)

        @pl.when(kc + W_SLOTS < n_k)
        def _():
            for cp in in_copies(e, kc + W_SLOTS):
                cp.start()

    def land_out(e, c):
        out_copy(e, c).wait()
        wb_d[c % 2] = st_d[c % W_SLOTS].astype(BF16)

        @pl.when(c + W_SLOTS < n_c)
        def _():
            out_copy(e, c + W_SLOTS).start()

    def gate_up(kc):
        x_k = xs[:, pl.ds(pl.multiple_of(kc * ck, ck), ck)]
        acc_a[...] += _dot(x_k, wb_g[kc % 2])
        acc_u[...] += _dot(x_k, wb_u[kc % 2])

    def down(c):
        yst[c % 2] = _dot(hid[...], wb_d[c % 2])
        y_copy(c).start()

    @pl.when(s == 0)
    def _():
        xs[...] = jnp.zeros_like(xs)
        start_gather(0)

    @pl.when(s < n_seg)
    def _():
        e = expert_base + se_ref[s]
        for i in range(W_SLOTS):
            for cp in in_copies(e, i):
                cp.start()
            out_copy(e, i).start()
        finish_gather(s)

        @pl.when(s + 1 < n_seg)
        def _():
            start_gather(s + 1)

        acc_a[...] = jnp.zeros_like(acc_a)
        acc_u[...] = jnp.zeros_like(acc_u)
        land_in(e, 0)

        def in_step(kc, carry):
            land_in(e, kc + 1)
            gate_up(kc)
            return carry

        lax.fori_loop(0, n_k - 1, in_step, 0)
        land_out(e, 0)
        gate_up(n_k - 1)
        hid[...] = (_silu(acc_a[...]) * acc_u[...]).astype(BF16)

        def out_step(c, carry):
            @pl.when(c >= 2)
            def _():
                y_copy(c - 2).wait()

            land_out(e, c + 1)
            down(c)
            return carry

        lax.fori_loop(0, n_c - 1, out_step, 0)
        y_copy(n_c - 3).wait()
        down(n_c - 1)
        y_copy(n_c - 2).wait()
        y_copy(n_c - 1).wait()

    @pl.when(fill_ref[s] == 1)
    def _():
        yst[0] = jnp.zeros_like(yst[0])
        for c in range(n_c):
            y_copy(c, 0).start()
        for c in range(n_c):
            y_copy(c, 0).wait()


def expert_mlp(h, plan, wg, wu, wd, *, expert_base):
    src, _, seg_expert, seg_row0, seg_nt, seg_fill, n_seg = plan
    d = h.shape[1]
    f = wg.shape[2]
    n_grid = seg_expert.shape[0]
    ck = d // W_CHUNKS
    cn = d // W_CHUNKS
    return pl.pallas_call(
        functools.partial(_expert_kernel, expert_base=expert_base),
        out_shape=jax.ShapeDtypeStruct((src.shape[0] + SEG_ROWS, d), F32),
        grid_spec=pltpu.PrefetchScalarGridSpec(
            num_scalar_prefetch=6,
            grid=(n_grid,),
            in_specs=[pl.BlockSpec(memory_space=pl.ANY)] * 4,
            out_specs=pl.BlockSpec(memory_space=pl.ANY),
            scratch_shapes=[
                pltpu.VMEM((SEG_ROWS, d), F32), pltpu.VMEM((SEG_ROWS, d), BF16),
                pltpu.VMEM((SEG_ROWS, f), F32), pltpu.VMEM((SEG_ROWS, f), F32), pltpu.VMEM((SEG_ROWS, f), BF16),
                pltpu.VMEM((W_SLOTS, ck, f), F32), pltpu.VMEM((W_SLOTS, ck, f), F32), pltpu.VMEM((W_SLOTS, f, cn), F32),
                pltpu.VMEM((2, ck, f), BF16), pltpu.VMEM((2, ck, f), BF16), pltpu.VMEM((2, f, cn), BF16),
                pltpu.VMEM((2, SEG_ROWS, cn), F32),
                pltpu.SemaphoreType.DMA, pltpu.SemaphoreType.DMA((W_SLOTS,)), pltpu.SemaphoreType.DMA((W_SLOTS,)),
                pltpu.SemaphoreType.DMA((W_SLOTS,)), pltpu.SemaphoreType.DMA((2,)),
            ],
        ),
        compiler_params=_cparams(("arbitrary",)),
        name="moe_experts",
    )(seg_expert, seg_row0, seg_nt, seg_fill, n_seg, src, h, wg, wu, wd)


def _combine_kernel(p1_ref, p2_ref, x_ref, g_ref, r_ref, fg_ref, y_hbm, o_ref, buf_ref, sem, *, final_norm):
    i = pl.program_id(0)
    rows = x_ref.shape[0]

    def start(r, carry):
        _row_copy(y_hbm, p1_ref[i * rows + r], buf_ref.at[0], r, sem).start()
        _row_copy(y_hbm, p2_ref[i * rows + r], buf_ref.at[1], r, sem).start()
        return carry

    lax.fori_loop(0, rows, start, 0, unroll=8)
    pltpu.make_async_copy(y_hbm.at[pl.ds(0, rows)], buf_ref.at[0], sem).wait()
    pltpu.make_async_copy(y_hbm.at[pl.ds(0, rows)], buf_ref.at[1], sem).wait()
    w1 = r_ref[:, 2:3]
    w2 = r_ref[:, 3:4]
    out = x_ref[...] + g_ref[0] * (w1 * buf_ref[0] + w2 * buf_ref[1])
    o_ref[...] = _rms(out, fg_ref[...]) if final_norm else out


def moe_combine(x, y, pos, routed, mod3, gate_chunk, mod_row, final_g=None):
    t, d = x.shape
    fg = jnp.ones((1, d), F32) if final_g is None else final_g.reshape(1, d)
    return pl.pallas_call(
        functools.partial(_combine_kernel, final_norm=final_g is not None),
        out_shape=jax.ShapeDtypeStruct((t, d), F32),
        grid_spec=pltpu.PrefetchScalarGridSpec(
            num_scalar_prefetch=2,
            grid=(t // ROW_TILE,),
            in_specs=[
                pl.BlockSpec((ROW_TILE, d), lambda i, a, b: (i, 0)),
                pl.BlockSpec((1, 1, d), lambda i, a, b: (mod_row(i), 0, gate_chunk)),
                pl.BlockSpec((ROW_TILE, LANES), lambda i, a, b: (i, 0)),
                pl.BlockSpec((1, d), lambda i, a, b: (0, 0)),
                pl.BlockSpec(memory_space=pl.ANY),
            ],
            out_specs=pl.BlockSpec((ROW_TILE, d), lambda i, a, b: (i, 0)),
            scratch_shapes=[pltpu.VMEM((2, ROW_TILE, d), F32), pltpu.SemaphoreType.DMA],
        ),
        compiler_params=_cparams(("arbitrary",)),
        name="moe_combine",
    )(pos[:, 0], pos[:, 1], x, mod3, routed, fg, y)


def kernel(x, c, ctx, c_ctx, ada_w, ada_b, norm1_g, w_in, gla_wa_fwd, gla_ba_fwd, gla_wa_bwd, gla_ba_bwd, gla_norm_g,
           gqa_qnorm_g, gqa_knorm_g, diff_lq1, diff_lk1, diff_lq2, diff_lk2, diff_norm_g, w_out, norm2_g, router_wg,
           router_bg, router_we, router_be, moe_w_gate, moe_w_up, moe_w_down, final_norm_g):
    nb, seq, d = x.shape
    n_ctx = ctx.shape[1]
    depth = ada_w.shape[0]
    assert n_ctx == ROW_TILE and seq % ROW_TILE == 0 and seq % GRID_W == 0
    rpb = n_ctx + seq
    tpb = rpb // ROW_TILE
    t = nb * rpb

    gla_heads = (d // 4) // GLA_DV
    gqa_heads = (d // 2) // HEAD_DIM
    n_kv = gqa_heads // GQA_GROUP
    diff_heads = (d // 4) // DIFF_DV
    kw = gla_heads * GLA_DK
    vw = gla_heads * GLA_DV
    splits = (kw, kw, vw, vw, GLA_RANK, GLA_RANK, gqa_heads * HEAD_DIM, n_kv * HEAD_DIM, n_kv * HEAD_DIM,
              diff_heads * 2 * HEAD_DIM, diff_heads * 2 * HEAD_DIM, diff_heads * DIFF_DV)
    offs = [0]
    for s in splits:
        offs.append(offs[-1] + s)
    col = {}
    acc = 0
    for name, width in (("aq", kw), ("ak", kw), ("av", vw), ("ag", vw), ("bq", splits[6]), ("bk", splits[7]),
                        ("bv", splits[8]), ("dq", splits[9]), ("dk", splits[10]), ("dv", splits[11])):
        col[name] = acc
        acc += width

    def mod_row_full(i):
        return jnp.where(i % tpb == 0, nb, i // tpb)

    def mod_row_latent(i):
        return i // (tpb - 1)

    tokens = jnp.concatenate([ctx, x], axis=1).reshape(t, d)
    c_all = jnp.zeros((8, d), F32).at[:nb].set(c).at[nb].set(c_ctx)
    mod = ada_modulation(c_all, ada_w, ada_b)
    cos, sin = rope_tables(n_ctx, seq)
    w_wide = drop_cols_bf16(w_in, offs[4], offs[6])
    w_out_bf = w_out.astype(BF16)
    f = moe_w_gate.shape[-1]
    wg_all = moe_w_gate.reshape(depth * N_EXPERTS, d, f)
    wu_all = moe_w_up.reshape(depth * N_EXPERTS, d, f)
    wd_all = moe_w_down.reshape(depth * N_EXPERTS, f, d)

    for l in range(depth):
        last = l == depth - 1
        lam_init = 0.8 - 0.6 * math.exp(-0.3 * l)
        mod3 = mod[l].reshape(8, 1, 6 * d)

        w_dec = jnp.zeros((d, LANES), F32).at[:, :2 * GLA_RANK].set(w_in[l, :, offs[4]:offs[6]])
        h, paa = norm_modulate(tokens, norm1_g[l], mod3, 1, 0, w_dec, mod_row_full, h_dtype=BF16, exact_small=False)
        p = matmul(h, w_wide, l, bm=1024, bn=1024, out_dtype=BF16)

        wa_blk = jnp.zeros((2, LANES, kw), F32)
        wa_blk = wa_blk.at[0, :GLA_RANK].set(gla_wa_fwd[l]).at[1, GLA_RANK:2 * GLA_RANK].set(gla_wa_bwd[l])
        ba_blk = jnp.stack([gla_ba_fwd[l], gla_ba_bwd[l]]).reshape(2, 1, kw)
        o_f, o_b = gla_scan(p, paa, wa_blk, ba_blk, nb=nb, rows_per_batch=rpb, n_ctx_chunks=n_ctx // GLA_CHUNK,
                            nh=gla_heads)
        if last:
            x_tile, mod_row = _LatentTiles(nb, tpb), mod_row_latent
        else:
            x_tile, mod_row = _AllTiles(t // ROW_TILE), mod_row_full
        mix_a = gla_output(o_f, o_b, p, gla_norm_g[l], x_tile, gate_block=col["ag"] // vw, nh=gla_heads)
        mix_b = gqa_attention(p, cos, sin, gqa_qnorm_g[l], gqa_knorm_g[l], nb=nb, rows_per_batch=rpb, n_ctx=n_ctx,
                              n_kv=n_kv, q_col=col["bq"], k_col=col["bk"], v_col=col["bv"], with_ctx=not last)
        mix_c = diff_attention(p, cos, sin, diff_lq1[l], diff_lk1[l], diff_lq2[l], diff_lk2[l], diff_norm_g[l],
                               nb=nb, rows_per_batch=rpb, n_ctx=n_ctx, nh=diff_heads, q_col=col["dq"],
                               k_col=col["dk"], v_col=col["dv"], with_ctx=not last, lam_init=lam_init)
        tokens = matmul3_gated_residual(mix_a, mix_b, mix_c, w_out_bf, l, tokens, mod3, 2, x_tile, mod_row, bn=1024)

        w_route = jnp.zeros((d, LANES), F32).at[:, :N_GROUPS].set(router_wg[l])
        w_route = w_route.at[:, N_GROUPS:N_GROUPS + N_EXPERTS].set(router_we[l])
        b_route = jnp.zeros((1, LANES), F32).at[0, :N_GROUPS].set(router_bg[l])
        b_route = b_route.at[0, N_GROUPS:N_GROUPS + N_EXPERTS].set(router_be[l])
        h2, logits = norm_modulate(tokens, norm2_g[l], mod3, 4, 3, w_route, mod_row, h_dtype=F32, exact_small=True)
        routed = route(logits, b_route)
        plan = dispatch_plan(routed[:, 0:2].astype(jnp.int32))
        y = expert_mlp(h2, plan, wg_all, wu_all, wd_all, expert_base=l * N_EXPERTS)
        tokens = moe_combine(tokens, y, plan[1], routed, mod3, 5, mod_row, final_norm_g if last else None)

    return tokens.reshape(nb, seq, d)
```

```python
import functools
import math

import jax
import jax.numpy as jnp
from jax import lax
from jax.experimental import pallas as pl
from jax.experimental.pallas import tpu as pltpu

F32 = jnp.float32
BF16 = jnp.bfloat16

HEAD_DIM = 128
GRID_W = 64
ROPE_THETA = 10000.0
EPS = 1e-6
GLA_DK = HEAD_DIM
GLA_DV = 2 * HEAD_DIM
GLA_RANK = 16
GLA_TAU = 16.0
GLA_CHUNK = 64
GQA_GROUP = 4
GQA_KV_PER_STEP = 2
DIFF_DV = 2 * HEAD_DIM
N_GROUPS = 4
EXPERTS_PER_GROUP = 8
N_EXPERTS = N_GROUPS * EXPERTS_PER_GROUP

LANES = 128
ROW_TILE = 256
SEG_TILE = 128
SEG_ROWS = 768
W_CHUNKS = 8
W_SLOTS = 3
VMEM_LIMIT = 56 * 1024 * 1024


def _cparams(sem, vmem=VMEM_LIMIT):
    return pltpu.CompilerParams(dimension_semantics=sem, vmem_limit_bytes=vmem)


def _sigmoid(x):
    return 1.0 / (1.0 + jnp.exp(-x))


def _silu(x):
    return x * _sigmoid(x)


def _log_sigmoid(x):
    return jnp.minimum(x, 0.0) - jnp.log(1.0 + jnp.exp(-jnp.abs(x)))


def _rms(x, g):
    return x * lax.rsqrt(jnp.mean(x * x, axis=-1, keepdims=True) + EPS) * g


def _dot(a, b):
    return jnp.dot(a, b, preferred_element_type=F32)


def _dot_nt(a, b):
    return lax.dot_general(a, b, (((1,), (1,)), ((), ())), preferred_element_type=F32)


def _dot_tn(a, b):
    return lax.dot_general(a, b, (((0,), (0,)), ((), ())), preferred_element_type=F32)


class _AllTiles:
    def __init__(self, n):
        self.n = n

    def __call__(self, i):
        return i


class _LatentTiles:
    def __init__(self, nb, tiles_per_batch):
        self.n = nb * (tiles_per_batch - 1)
        self._lat = tiles_per_batch - 1
        self._tpb = tiles_per_batch

    def __call__(self, i):
        return (i // self._lat) * self._tpb + 1 + i % self._lat


def _ada_kernel(c_ref, w_ref, b_ref, o_ref):
    s = _silu(c_ref[...]).astype(BF16)
    o_ref[0] = _dot(s, w_ref[0].astype(BF16)) + b_ref[0]


def ada_modulation(c_all, ada_w, ada_b, *, tn=512):
    nl, d, n = ada_w.shape
    return pl.pallas_call(
        _ada_kernel,
        out_shape=jax.ShapeDtypeStruct((nl, 8, n), F32),
        grid=(nl, n // tn),
        in_specs=[
            pl.BlockSpec((8, d), lambda l, j: (0, 0)),
            pl.BlockSpec((1, d, tn), lambda l, j: (l, 0, j)),
            pl.BlockSpec((1, 1, tn), lambda l, j: (l, 0, j)),
        ],
        out_specs=pl.BlockSpec((1, 8, tn), lambda l, j: (l, 0, j)),
        compiler_params=_cparams(("arbitrary", "arbitrary")),
        name="ada_modulation",
    )(c_all, ada_w, ada_b.reshape(nl, 1, n))


def _norm_kernel(x_ref, g_ref, sc_ref, sh_ref, w_ref, h_ref, s_ref, *, exact_small):
    h = _rms(x_ref[...], g_ref[...]) * (1.0 + sc_ref[0]) + sh_ref[0]
    h_ref[...] = h.astype(h_ref.dtype)
    if exact_small:
        s_ref[...] = jnp.dot(h, w_ref[...], precision=lax.Precision.HIGHEST, preferred_element_type=F32)
    else:
        s_ref[...] = _dot(h.astype(BF16), w_ref[...].astype(BF16))


def norm_modulate(x, g, mod3, sc_chunk, sh_chunk, w_small, mod_row, *, h_dtype, exact_small):
    t, d = x.shape
    ns = w_small.shape[1]
    return pl.pallas_call(
        functools.partial(_norm_kernel, exact_small=exact_small),
        out_shape=(jax.ShapeDtypeStruct((t, d), h_dtype), jax.ShapeDtypeStruct((t, ns), F32)),
        grid=(t // ROW_TILE,),
        in_specs=[
            pl.BlockSpec((ROW_TILE, d), lambda i: (i, 0)),
            pl.BlockSpec((1, d), lambda i: (0, 0)),
            pl.BlockSpec((1, 1, d), lambda i: (mod_row(i), 0, sc_chunk)),
            pl.BlockSpec((1, 1, d), lambda i: (mod_row(i), 0, sh_chunk)),
            pl.BlockSpec((d, ns), lambda i: (0, 0)),
        ],
        out_specs=(pl.BlockSpec((ROW_TILE, d), lambda i: (i, 0)), pl.BlockSpec((ROW_TILE, ns), lambda i: (i, 0))),
        compiler_params=_cparams(("arbitrary",)),
        name="norm_modulate",
    )(x, g.reshape(1, d), mod3, mod3, w_small)


def _transpose_cast_kernel(w_ref, o_ref):
    o_ref[0] = w_ref[0].T.astype(o_ref.dtype)


def drop_rows_transpose_bf16(w_t, lo, hi, *, bn=1024):
    nl, n, k = w_t.shape
    n_out = n - (hi - lo)
    assert lo % bn == 0 and n_out % bn == 0 and (hi - lo) % 8 == 0

    def src_row(j):
        return pl.multiple_of(j * bn + jnp.where(j * bn >= lo, hi - lo, 0), 8)

    return pl.pallas_call(
        _transpose_cast_kernel,
        out_shape=jax.ShapeDtypeStruct((nl, k, n_out), BF16),
        grid=(nl, n_out // bn),
        in_specs=[pl.BlockSpec((pl.Element(1), pl.Element(bn), pl.Element(k)), lambda l, j: (l, src_row(j), 0))],
        out_specs=pl.BlockSpec((1, k, bn), lambda l, j: (l, 0, j)),
        compiler_params=_cparams(("arbitrary", "arbitrary")),
        name="drop_rows_transpose_bf16",
    )(w_t)


def _mm_kernel(a_ref, b_ref, o_ref):
    o_ref[...] = _dot(a_ref[...], b_ref[...]).astype(o_ref.dtype)


def matmul(a, b_all, layer, *, bm, bn, out_dtype):
    m, k = a.shape
    n = b_all.shape[2]
    return pl.pallas_call(
        _mm_kernel,
        out_shape=jax.ShapeDtypeStruct((m, n), out_dtype),
        grid=(m // bm, n // bn),
        in_specs=[pl.BlockSpec((bm, k), lambda i, j: (i, 0)), pl.BlockSpec((None, k, bn), lambda i, j: (layer, 0, j))],
        out_specs=pl.BlockSpec((bm, bn), lambda i, j: (i, j)),
        compiler_params=_cparams(("arbitrary", "arbitrary")),
        name="matmul",
    )(a, b_all)


def _mm3_res_kernel(a1_ref, a2_ref, a3_ref, b_ref, x_ref, g_ref, o_ref):
    k1 = a1_ref.shape[1]
    k2 = k1 + a2_ref.shape[1]
    acc = _dot(a1_ref[...], b_ref[0:k1, :]) + _dot(a2_ref[...], b_ref[k1:k2, :]) + _dot(a3_ref[...], b_ref[k2:, :])
    o_ref[...] = x_ref[...] + g_ref[0] * acc


def matmul3_gated_residual(a1, a2, a3, b_all, layer, x, mod3, gate_chunk, x_tile, mod_row, *, bn):
    k = b_all.shape[1]
    n = b_all.shape[2]
    nbn = n // bn
    return pl.pallas_call(
        _mm3_res_kernel,
        out_shape=jax.ShapeDtypeStruct((x_tile.n * ROW_TILE, n), F32),
        grid=(nbn, x_tile.n),
        in_specs=[
            pl.BlockSpec((ROW_TILE, a1.shape[1]), lambda j, i: (i, 0)),
            pl.BlockSpec((ROW_TILE, a2.shape[1]), lambda j, i: (i, 0)),
            pl.BlockSpec((ROW_TILE, a3.shape[1]), lambda j, i: (i, 0)),
            pl.BlockSpec((None, k, bn), lambda j, i: (layer, 0, j)),
            pl.BlockSpec((ROW_TILE, bn), lambda j, i: (x_tile(i), j)),
            pl.BlockSpec((1, 1, bn), lambda j, i: (mod_row(i), 0, gate_chunk * nbn + j)),
        ],
        out_specs=pl.BlockSpec((ROW_TILE, bn), lambda j, i: (i, j)),
        compiler_params=_cparams(("arbitrary", "arbitrary")),
        name="matmul3_gated_residual",
    )(a1, a2, a3, b_all, x, mod3)


def _gla_direction(q_ref, k_ref, v_ref, paa_ref, wa, ba, o_ref, st_ref, *, nh, backward):
    c = GLA_CHUNK
    row = lax.broadcasted_iota(jnp.int32, (c, c), 0)
    col = lax.broadcasted_iota(jnp.int32, (c, c), 1)
    incl = (col >= row) if backward else (col <= row)
    tri = incl.astype(BF16)

    z = _dot(paa_ref[...].astype(BF16), wa.astype(BF16)) + ba
    la = _log_sigmoid(z) * (1.0 / GLA_TAU)
    la_hi = la.astype(BF16)
    la_lo = (la - la_hi.astype(F32)).astype(BF16)
    b = _dot(tri, la_hi) + _dot(tri, la_lo)
    tot = jnp.sum(la, axis=0, keepdims=True)
    mid = 0.5 * tot
    e_q = jnp.exp(b - mid)
    e_k = jnp.exp(mid - b)
    e_in = jnp.exp(b)
    e_out = jnp.exp(tot - b)
    e_tot = jnp.exp(tot)

    for h in range(nh):
        ks = slice(h * GLA_DK, (h + 1) * GLA_DK)
        vs = slice(h * GLA_DV, (h + 1) * GLA_DV)
        q = q_ref[:, ks].astype(F32) * (GLA_DK ** -0.5)
        k = k_ref[:, ks].astype(F32)
        v = v_ref[:, vs]
        s = _dot_nt((q * e_q[:, ks]).astype(BF16), (k * e_k[:, ks]).astype(BF16))
        s = jnp.where(incl, s, 0.0)
        st = st_ref[h]
        o = _dot(s.astype(BF16), v) + _dot_nt((q * e_in[:, ks]).astype(BF16), st.astype(BF16))
        st_ref[h] = st * e_tot[:, ks] + _dot_tn(v, (k * e_out[:, ks]).astype(BF16))
        o_ref[:, vs] = o


def _gla_kernel(qf, kf, vf, pf, qb, kb, vb, pb, wa_ref, ba_ref, of_ref, ob_ref, st_ref, *, nh):
    @pl.when(pl.program_id(1) == 0)
    def _():
        st_ref[...] = jnp.zeros_like(st_ref)

    _gla_direction(qf, kf, vf, pf, wa_ref[0], ba_ref[0], of_ref, st_ref.at[0], nh=nh, backward=False)
    _gla_direction(qb, kb, vb, pb, wa_ref[1], ba_ref[1], ob_ref, st_ref.at[1], nh=nh, backward=True)


def gla_scan(p, paa, wa_blk, ba_blk, *, nb, rows_per_batch, n_ctx_chunks, nh):
    t = p.shape[0]
    c = GLA_CHUNK
    ncb = rows_per_batch // c
    kw = nh * GLA_DK
    vw = nh * GLA_DV

    def fwd(b, g):
        return b * ncb + g

    def bwd(b, g):
        return b * ncb + jnp.where(g < n_ctx_chunks, n_ctx_chunks - 1 - g, ncb - 1 + n_ctx_chunks - g)

    def chunk_specs(cm):
        return [
            pl.BlockSpec((c, kw), lambda b, g: (cm(b, g), 0)),
            pl.BlockSpec((c, kw), lambda b, g: (cm(b, g), 1)),
            pl.BlockSpec((c, vw), lambda b, g: (cm(b, g), 2 * kw // vw)),
            pl.BlockSpec((c, LANES), lambda b, g: (cm(b, g), 0)),
        ]

    out = jax.ShapeDtypeStruct((t, vw), F32)
    return pl.pallas_call(
        functools.partial(_gla_kernel, nh=nh),
        out_shape=(out, out),
        grid=(nb, ncb),
        in_specs=chunk_specs(fwd) + chunk_specs(bwd) + [
            pl.BlockSpec((2, LANES, kw), lambda b, g: (0, 0, 0)),
            pl.BlockSpec((2, 1, kw), lambda b, g: (0, 0, 0)),
        ],
        out_specs=(pl.BlockSpec((c, vw), lambda b, g: (fwd(b, g), 0)), pl.BlockSpec((c, vw), lambda b, g: (bwd(b, g), 0))),
        scratch_shapes=[pltpu.VMEM((2, nh, GLA_DV, GLA_DK), F32)],
        compiler_params=_cparams(("arbitrary", "arbitrary")),
        name="gla_scan",
    )(p, p, p, paa, p, p, p, paa, wa_blk, ba_blk)


def _gla_out_kernel(of_ref, ob_ref, gate_ref, gn_ref, out_ref, *, nh):
    o = of_ref[...] + ob_ref[...]
    for h in range(nh):
        vs = slice(h * GLA_DV, (h + 1) * GLA_DV)
        gate = gate_ref[:, vs].astype(F32)
        out_ref[:, vs] = (_rms(o[:, vs], gn_ref[...]) * _silu(gate)).astype(out_ref.dtype)


def gla_output(o_f, o_b, p, gn, in_tile, *, gate_block, nh):
    vw = nh * GLA_DV
    row = pl.BlockSpec((ROW_TILE, vw), lambda i: (in_tile(i), 0))
    return pl.pallas_call(
        functools.partial(_gla_out_kernel, nh=nh),
        out_shape=jax.ShapeDtypeStruct((in_tile.n * ROW_TILE, vw), BF16),
        grid=(in_tile.n,),
        in_specs=[row, row, pl.BlockSpec((ROW_TILE, vw), lambda i: (in_tile(i), gate_block)),
                  pl.BlockSpec((1, GLA_DV), lambda i: (0, 0))],
        out_specs=pl.BlockSpec((ROW_TILE, vw), lambda i: (i, 0)),
        compiler_params=_cparams(("arbitrary",)),
        name="gla_output",
    )(o_f, o_b, p, gn.reshape(1, GLA_DV))


def _rope(x, cos, sin_signed):
    lane = lax.broadcasted_iota(jnp.int32, x.shape, 1)
    first = (lane % (HEAD_DIM // 2)) < (HEAD_DIM // 4)
    rot = jnp.where(first, pltpu.roll(x, HEAD_DIM - HEAD_DIM // 4, 1), pltpu.roll(x, HEAD_DIM // 4, 1))
    return x * cos + rot * sin_signed


def rope_tables(n_ctx, seq):
    nf = HEAD_DIM // 4
    rows = seq // GRID_W
    row = jnp.repeat(jnp.arange(rows, dtype=jnp.int32), GRID_W).astype(F32)
    col = jnp.tile(jnp.arange(GRID_W, dtype=jnp.int32), rows).astype(F32)
    inv = ROPE_THETA ** (-jnp.arange(nf, dtype=F32) / nf)
    ang = jnp.concatenate([row[:, None] * inv, row[:, None] * inv, col[:, None] * inv, col[:, None] * inv], axis=1)
    sign = jnp.tile(jnp.concatenate([-jnp.ones((nf,), F32), jnp.ones((nf,), F32)]), 2)
    cos = jnp.concatenate([jnp.ones((n_ctx, HEAD_DIM), F32), jnp.cos(ang)], axis=0)
    sin = jnp.concatenate([jnp.zeros((n_ctx, HEAD_DIM), F32), jnp.sin(ang) * sign], axis=0)
    return cos, sin


def _attend(q, keys, vals):
    s = _dot_nt(q, keys)
    e = jnp.exp(s - jnp.max(s, axis=-1, keepdims=True))
    return _dot(e.astype(BF16), vals), jnp.sum(e, axis=-1, keepdims=True)


def _gqa_kernel(q_ref, k_ref, v_ref, cq_ref, sq_ref, ck_ref, sk_ref, qn_ref, kn_ref, o_ref, ks_ref, *,
                n_ctx, ctx_tile_first):
    qi = pl.program_id(2)

    n_kv = k_ref.shape[1] // HEAD_DIM

    @pl.when(qi == 0)
    def _():
        for j in range(n_kv):
            cs = slice(j * HEAD_DIM, (j + 1) * HEAD_DIM)
            k = _rms(k_ref[:, cs].astype(F32), kn_ref[...])
            ks_ref[:, cs] = _rope(k, ck_ref[...], sk_ref[...]).astype(BF16)

    def attend(nk):
        for j in range(n_kv):
            cs = slice(j * HEAD_DIM, (j + 1) * HEAD_DIM)
            vals = v_ref[0:nk, cs]
            for g in range(GQA_GROUP):
                head = j * GQA_GROUP + g
                hs = slice(head * HEAD_DIM, (head + 1) * HEAD_DIM)
                q = _rms(q_ref[:, hs].astype(F32), qn_ref[...])
                q = (_rope(q, cq_ref[...], sq_ref[...]) * (HEAD_DIM ** -0.5)).astype(BF16)
                pv, l = _attend(q, ks_ref[0:nk, cs], vals)
                o_ref[:, hs] = (pv / l).astype(o_ref.dtype)

    if ctx_tile_first:
        @pl.when(qi == 0)
        def _():
            attend(n_ctx)

        @pl.when(qi > 0)
        def _():
            attend(k_ref.shape[0])
    else:
        attend(k_ref.shape[0])


def gqa_attention(p, cos, sin, qn, kn, *, nb, rows_per_batch, n_ctx, n_kv, q_col, k_col, v_col, with_ctx):
    t = p.shape[0]
    tpb = rows_per_batch // ROW_TILE
    nq = tpb if with_ctx else tpb - 1
    off = 0 if with_ctx else 1
    kvw = GQA_KV_PER_STEP * HEAD_DIM
    qw = GQA_GROUP * kvw
    return pl.pallas_call(
        functools.partial(_gqa_kernel, n_ctx=n_ctx, ctx_tile_first=with_ctx),
        out_shape=jax.ShapeDtypeStruct((nb * nq * ROW_TILE, n_kv * GQA_GROUP * HEAD_DIM), BF16),
        grid=(nb, n_kv // GQA_KV_PER_STEP, nq),
        in_specs=[
            pl.BlockSpec((ROW_TILE, qw), lambda b, h, i: (b * tpb + i + off, q_col // qw + h)),
            pl.BlockSpec((rows_per_batch, kvw), lambda b, h, i: (b, k_col // kvw + h)),
            pl.BlockSpec((rows_per_batch, kvw), lambda b, h, i: (b, v_col // kvw + h)),
            pl.BlockSpec((ROW_TILE, HEAD_DIM), lambda b, h, i: (i + off, 0)),
            pl.BlockSpec((ROW_TILE, HEAD_DIM), lambda b, h, i: (i + off, 0)),
            pl.BlockSpec((rows_per_batch, HEAD_DIM), lambda b, h, i: (0, 0)),
            pl.BlockSpec((rows_per_batch, HEAD_DIM), lambda b, h, i: (0, 0)),
            pl.BlockSpec((1, HEAD_DIM), lambda b, h, i: (0, 0)),
            pl.BlockSpec((1, HEAD_DIM), lambda b, h, i: (0, 0)),
        ],
        out_specs=pl.BlockSpec((ROW_TILE, qw), lambda b, h, i: (b * nq + i, h)),
        scratch_shapes=[pltpu.VMEM((rows_per_batch, kvw), BF16)],
        compiler_params=_cparams(("arbitrary", "arbitrary", "arbitrary")),
        name="gqa_attention",
    )(p, p, p, cos, sin, cos, sin, qn.reshape(1, HEAD_DIM), kn.reshape(1, HEAD_DIM))


def _diff_kernel(q_ref, k_ref, v_ref, cq_ref, sq_ref, ck_ref, sk_ref, lq1_ref, lk1_ref, lq2_ref, lk2_ref, gn_ref,
                 o_ref, ks_ref, *, n_ctx, ctx_tile_first, lam_init):
    qi = pl.program_id(1)
    nh = v_ref.shape[1] // DIFF_DV

    @pl.when(qi == 0)
    def _():
        for j in range(2 * nh):
            hs = slice(j * HEAD_DIM, (j + 1) * HEAD_DIM)
            ks_ref[:, hs] = _rope(k_ref[:, hs].astype(F32), ck_ref[...], sk_ref[...]).astype(BF16)

    lam = (jnp.exp(jnp.sum(lq1_ref[...] * lk1_ref[...], axis=-1, keepdims=True))
           - jnp.exp(jnp.sum(lq2_ref[...] * lk2_ref[...], axis=-1, keepdims=True)) + lam_init)

    def attend(nk):
        for h in range(nh):
            vs = slice(h * DIFF_DV, (h + 1) * DIFF_DV)
            vals = v_ref[0:nk, vs]
            o = None
            for j in range(2):
                hs = slice((2 * h + j) * HEAD_DIM, (2 * h + j + 1) * HEAD_DIM)
                q = (_rope(q_ref[:, hs].astype(F32), cq_ref[...], sq_ref[...]) * (HEAD_DIM ** -0.5)).astype(BF16)
                pv, l = _attend(q, ks_ref[0:nk, hs], vals)
                o = pv / l if j == 0 else o - (lam / l) * pv
            o_ref[:, vs] = (_rms(o, gn_ref[...]) * (1.0 - lam_init)).astype(o_ref.dtype)

    if ctx_tile_first:
        @pl.when(qi == 0)
        def _():
            attend(n_ctx)

        @pl.when(qi > 0)
        def _():
            attend(k_ref.shape[0])
    else:
        attend(k_ref.shape[0])


def diff_attention(p, cos, sin, lq1, lk1, lq2, lk2, gn, *, nb, rows_per_batch, n_ctx, nh, q_col, k_col, v_col,
                   with_ctx, lam_init):
    t = p.shape[0]
    tpb = rows_per_batch // ROW_TILE
    nq = tpb if with_ctx else tpb - 1
    off = 0 if with_ctx else 1
    w = nh * DIFF_DV
    vec = pl.BlockSpec((1, HEAD_DIM), lambda b, i: (0, 0))
    return pl.pallas_call(
        functools.partial(_diff_kernel, n_ctx=n_ctx, ctx_tile_first=with_ctx, lam_init=lam_init),
        out_shape=jax.ShapeDtypeStruct((nb * nq * ROW_TILE, w), BF16),
        grid=(nb, nq),
        in_specs=[
            pl.BlockSpec((ROW_TILE, w), lambda b, i: (b * tpb + i + off, q_col // w)),
            pl.BlockSpec((rows_per_batch, w), lambda b, i: (b, k_col // w)),
            pl.BlockSpec((rows_per_batch, w), lambda b, i: (b, v_col // w)),
            pl.BlockSpec((ROW_TILE, HEAD_DIM), lambda b, i: (i + off, 0)),
            pl.BlockSpec((ROW_TILE, HEAD_DIM), lambda b, i: (i + off, 0)),
            pl.BlockSpec((rows_per_batch, HEAD_DIM), lambda b, i: (0, 0)),
            pl.BlockSpec((rows_per_batch, HEAD_DIM), lambda b, i: (0, 0)),
            vec, vec, vec, vec,
            pl.BlockSpec((1, DIFF_DV), lambda b, i: (0, 0)),
        ],
        out_specs=pl.BlockSpec((ROW_TILE, w), lambda b, i: (b * nq + i, 0)),
        scratch_shapes=[pltpu.VMEM((rows_per_batch, w), BF16)],
        compiler_params=_cparams(("arbitrary", "arbitrary")),
        name="diff_attention",
    )(p, p, p, cos, sin, cos, sin, lq1.reshape(1, -1), lk1.reshape(1, -1), lq2.reshape(1, -1), lk2.reshape(1, -1),
      gn.reshape(1, DIFF_DV))


def _route_kernel(lg_ref, bias_ref, o_ref):
    x = lg_ref[...] + bias_ref[...]
    lane = lax.broadcasted_iota(jnp.int32, x.shape, 1).astype(F32)
    neg = -jnp.inf

    def first_max(vals, mask):
        v = jnp.where(mask, vals, neg)
        m = jnp.max(v, axis=-1, keepdims=True)
        idx = jnp.min(jnp.where(mask & (v == m), lane, float(LANES)), axis=-1, keepdims=True)
        return m, idx

    gmask = lane < N_GROUPS
    gm, gidx = first_max(x, gmask)
    g_w = 1.0 / jnp.sum(jnp.where(gmask, jnp.exp(x - gm), 0.0), axis=-1, keepdims=True)
    lo = N_GROUPS + EXPERTS_PER_GROUP * gidx
    emask = (lane >= lo) & (lane < lo + EXPERTS_PER_GROUP)
    m1, i1 = first_max(x, emask)
    m2, i2 = first_max(x, emask & (lane != i1))
    r = jnp.exp(m2 - m1)
    w1 = g_w / (1.0 + r)
    w2 = g_w * r / (1.0 + r)
    out = jnp.where(lane == 0, i1 - N_GROUPS, 0.0)
    out = jnp.where(lane == 1, i2 - N_GROUPS, out)
    out = jnp.where(lane == 2, w1, out)
    out = jnp.where(lane == 3, w2, out)
    o_ref[...] = out


def route(logits, bias):
    t = logits.shape[0]
    return pl.pallas_call(
        _route_kernel,
        out_shape=jax.ShapeDtypeStruct((t, LANES), F32),
        grid=(t // ROW_TILE,),
        in_specs=[pl.BlockSpec((ROW_TILE, LANES), lambda i: (i, 0)), pl.BlockSpec((1, LANES), lambda i: (0, 0))],
        out_specs=pl.BlockSpec((ROW_TILE, LANES), lambda i: (i, 0)),
        compiler_params=_cparams(("arbitrary",)),
        name="route",
    )(logits, bias)


def dispatch_plan(ids):
    t = ids.shape[0]
    tps = SEG_ROWS // SEG_TILE
    n_src = 2 * t + N_EXPERTS * SEG_TILE
    n_seg = (n_src // SEG_TILE + (tps - 1) * N_EXPERTS) // tps + 1 + (N_EXPERTS * SEG_TILE) // SEG_ROWS + 2
    e = ids.reshape(-1)
    onehot = (e[:, None] == jnp.arange(N_EXPERTS, dtype=jnp.int32)[None, :]).astype(jnp.int32)
    csum = jnp.cumsum(onehot, axis=0)
    counts = csum[-1]
    tiles = (counts + SEG_TILE - 1) // SEG_TILE
    segs = (tiles + tps - 1) // tps
    seg_end = jnp.cumsum(segs)
    seg_start = seg_end - segs
    row_start = (jnp.cumsum(tiles) - tiles) * SEG_TILE
    rank = csum - 1
    cpos = jnp.sum(onehot * (rank + row_start[None, :]), axis=1)
    src = jnp.zeros((n_src,), jnp.int32).at[cpos].set(jnp.arange(2 * t, dtype=jnp.int32) // 2)
    sidx = jnp.arange(n_seg, dtype=jnp.int32)
    seg_expert = jnp.minimum(jnp.sum((seg_end[None, :] <= sidx[:, None]).astype(jnp.int32), axis=1), N_EXPERTS - 1)
    sel = (seg_expert[:, None] == jnp.arange(N_EXPERTS, dtype=jnp.int32)[None, :]).astype(jnp.int32)
    k = sidx - jnp.sum(sel * seg_start[None, :], axis=1)
    seg_row0 = jnp.sum(sel * row_start[None, :], axis=1) + k * SEG_ROWS
    used = sidx < seg_end[-1]
    seg_nt = jnp.where(used, jnp.clip(jnp.sum(sel * tiles[None, :], axis=1) - k * tps, 0, tps), 0)
    y_rows = n_src + SEG_ROWS
    fill0 = jnp.max(jnp.where(used, seg_row0 + SEG_ROWS, 0)) + (sidx - seg_end[-1]) * SEG_ROWS
    seg_row0 = jnp.where(used, seg_row0, jnp.minimum(fill0, y_rows - SEG_ROWS))
    seg_fill = jnp.where(used, 0, (fill0 < y_rows).astype(jnp.int32))
    return src, cpos.reshape(t, 2), seg_expert, seg_row0, seg_nt, seg_fill, seg_end[-1:]


def _row_copy(src_hbm, idx, dst_vmem, r, sem):
    return pltpu.make_async_copy(src_hbm.at[pl.ds(idx, 1)], dst_vmem.at[pl.ds(r, 1)], sem)


def _expert_kernel(se_ref, r0_ref, nt_ref, fill_ref, nseg_ref, src_ref, h_hbm, wg_hbm, wu_hbm, wd_hbm, y_hbm,
                   land, xs, acc_a, acc_u, hid, st_g, st_u, st_d, wb_g, wb_u, wb_d, yst,
                   sem_x, sem_g, sem_u, sem_d, sem_y, *, expert_base):
    s = pl.program_id(0)
    n_seg = nseg_ref[0]
    d = xs.shape[1]
    ck = st_g.shape[1]
    cn = st_d.shape[2]
    n_k = d // ck
    n_c = d // cn
    assert n_k > W_SLOTS and n_c > W_SLOTS and n_c >= 3

    def start_gather(seg):
        r0 = r0_ref[seg]

        def tile(tt, carry):
            base = tt * SEG_TILE

            def body(r, c):
                _row_copy(h_hbm, src_ref[r0 + base + r], land, base + r, sem_x).start()
                return c

            return lax.fori_loop(0, SEG_TILE, body, carry, unroll=8)

        lax.fori_loop(0, nt_ref[seg], tile, 0)

    def finish_gather(seg):
        nt = nt_ref[seg]

        def wait(t, carry):
            pltpu.make_async_copy(h_hbm.at[pl.ds(0, SEG_TILE)], land.at[pl.ds(0, SEG_TILE)], sem_x).wait()
            return carry

        lax.fori_loop(0, nt, wait, 0)

        def convert(t, carry):
            rows = pl.ds(pl.multiple_of(t * SEG_TILE, SEG_TILE), SEG_TILE)
            xs[rows, :] = land[rows, :].astype(BF16)
            return carry

        lax.fori_loop(0, nt, convert, 0)

    def in_copies(e, kc):
        slot = kc % W_SLOTS
        rows = pl.ds(pl.multiple_of(kc * ck, ck), ck)
        return (pltpu.make_async_copy(wg_hbm.at[e, rows], st_g.at[slot], sem_g.at[slot]),
                pltpu.make_async_copy(wu_hbm.at[e, rows], st_u.at[slot], sem_u.at[slot]))

    def out_copy(e, c):
        slot = c % W_SLOTS
        cols = pl.ds(pl.multiple_of(c * cn, cn), cn)
        return pltpu.make_async_copy(wd_hbm.at[e, :, cols], st_d.at[slot], sem_d.at[slot])

    def y_copy(c, slot=None):
        slot = c % 2 if slot is None else slot
        cols = pl.ds(pl.multiple_of(c * cn, cn), cn)
        rows = pl.ds(pl.multiple_of(r0_ref[s], SEG_TILE), SEG_ROWS)
        return pltpu.make_async_copy(yst.at[slot], y_hbm.at[rows, cols], sem_y.at[slot])

    def land_in(e, kc):
        for cp in in_copies(e, kc):
            cp.wait()
        wb_g[kc % 2] = st_g[kc % W_SLOTS].astype(BF16)
        wb_u[kc % 2] = st_u[kc % W_SLOTS].astype(BF16)

        @pl.when(kc + W_SLOTS < n_k)
        def _():
            for cp in in_copies(e, kc + W_SLOTS):
                cp.start()

    def land_out(e, c):
        out_copy(e, c).wait()
        wb_d[c % 2] = st_d[c % W_SLOTS].astype(BF16)

        @pl.when(c + W_SLOTS < n_c)
        def _():
            out_copy(e, c + W_SLOTS).start()

    def gate_up(kc):
        x_k = xs[:, pl.ds(pl.multiple_of(kc * ck, ck), ck)]
        acc_a[...] += _dot(x_k, wb_g[kc % 2])
        acc_u[...] += _dot(x_k, wb_u[kc % 2])

    def down(c):
        yst[c % 2] = _dot(hid[...], wb_d[c % 2])
        y_copy(c).start()

    @pl.when(s == 0)
    def _():
        xs[...] = jnp.zeros_like(xs)
        start_gather(0)

    def start_first_in(seg):
        for i in range(W_SLOTS):
            for cp in in_copies(expert_base + se_ref[seg], i):
                cp.start()

    def start_first_out(seg):
        for i in range(W_SLOTS):
            out_copy(expert_base + se_ref[seg], i).start()

    @pl.when(s == 0)
    def _():
        start_first_in(0)
        start_first_out(0)

    @pl.when(s < n_seg)
    def _():
        e = expert_base + se_ref[s]
        finish_gather(s)

        @pl.when(s + 1 < n_seg)
        def _():
            start_gather(s + 1)

        acc_a[...] = jnp.zeros_like(acc_a)
        acc_u[...] = jnp.zeros_like(acc_u)
        land_in(e, 0)

        def in_step(kc, carry):
            land_in(e, kc + 1)
            gate_up(kc)
            return carry

        lax.fori_loop(0, n_k - 1, in_step, 0)

        @pl.when(s + 1 < n_seg)
        def _():
            start_first_in(s + 1)

        land_out(e, 0)
        gate_up(n_k - 1)
        hid[...] = (_silu(acc_a[...]) * acc_u[...]).astype(BF16)

        def out_step(c, carry):
            @pl.when(c >= 2)
            def _():
                y_copy(c - 2).wait()

            land_out(e, c + 1)
            down(c)
            return carry

        lax.fori_loop(0, n_c - 1, out_step, 0)

        @pl.when(s + 1 < n_seg)
        def _():
            start_first_out(s + 1)

        y_copy(n_c - 3).wait()
        down(n_c - 1)
        y_copy(n_c - 2).wait()
        y_copy(n_c - 1).wait()

    @pl.when(fill_ref[s] == 1)
    def _():
        yst[0] = jnp.zeros_like(yst[0])
        for c in range(n_c):
            y_copy(c, 0).start()
        for c in range(n_c):
            y_copy(c, 0).wait()


def expert_mlp(h, plan, wg, wu, wd, *, expert_base):
    src, _, seg_expert, seg_row0, seg_nt, seg_fill, n_seg = plan
    d = h.shape[1]
    f = wg.shape[2]
    n_grid = seg_expert.shape[0]
    ck = d // W_CHUNKS
    cn = d // W_CHUNKS
    return pl.pallas_call(
        functools.partial(_expert_kernel, expert_base=expert_base),
        out_shape=jax.ShapeDtypeStruct((src.shape[0] + SEG_ROWS, d), F32),
        grid_spec=pltpu.PrefetchScalarGridSpec(
            num_scalar_prefetch=6,
            grid=(n_grid,),
            in_specs=[pl.BlockSpec(memory_space=pl.ANY)] * 4,
            out_specs=pl.BlockSpec(memory_space=pl.ANY),
            scratch_shapes=[
                pltpu.VMEM((SEG_ROWS, d), F32), pltpu.VMEM((SEG_ROWS, d), BF16),
                pltpu.VMEM((SEG_ROWS, f), F32), pltpu.VMEM((SEG_ROWS, f), F32), pltpu.VMEM((SEG_ROWS, f), BF16),
                pltpu.VMEM((W_SLOTS, ck, f), F32), pltpu.VMEM((W_SLOTS, ck, f), F32), pltpu.VMEM((W_SLOTS, f, cn), F32),
                pltpu.VMEM((2, ck, f), BF16), pltpu.VMEM((2, ck, f), BF16), pltpu.VMEM((2, f, cn), BF16),
                pltpu.VMEM((2, SEG_ROWS, cn), F32),
                pltpu.SemaphoreType.DMA, pltpu.SemaphoreType.DMA((W_SLOTS,)), pltpu.SemaphoreType.DMA((W_SLOTS,)),
                pltpu.SemaphoreType.DMA((W_SLOTS,)), pltpu.SemaphoreType.DMA((2,)),
            ],
        ),
        compiler_params=_cparams(("arbitrary",)),
        name="moe_experts",
    )(seg_expert, seg_row0, seg_nt, seg_fill, n_seg, src, h, wg, wu, wd)


def _combine_kernel(p1_ref, p2_ref, x_ref, g_ref, r_ref, fg_ref, y_hbm, o_ref, buf_ref, sem, *, final_norm):
    i = pl.program_id(0)
    rows = x_ref.shape[0]
    slot = i % 2

    def start_tile(tile, sl):
        def start(r, carry):
            _row_copy(y_hbm, p1_ref[tile * rows + r], buf_ref.at[sl, 0], r, sem.at[sl]).start()
            _row_copy(y_hbm, p2_ref[tile * rows + r], buf_ref.at[sl, 1], r, sem.at[sl]).start()
            return carry

        lax.fori_loop(0, rows, start, 0, unroll=8)

    @pl.when(i == 0)
    def _():
        start_tile(0, 0)

    @pl.when(i + 1 < pl.num_programs(0))
    def _():
        start_tile(i + 1, 1 - slot)

    pltpu.make_async_copy(y_hbm.at[pl.ds(0, rows)], buf_ref.at[slot, 0], sem.at[slot]).wait()
    pltpu.make_async_copy(y_hbm.at[pl.ds(0, rows)], buf_ref.at[slot, 1], sem.at[slot]).wait()
    w1 = r_ref[:, 2:3]
    w2 = r_ref[:, 3:4]
    out = x_ref[...] + g_ref[0] * (w1 * buf_ref[slot, 0] + w2 * buf_ref[slot, 1])
    o_ref[...] = _rms(out, fg_ref[...]) if final_norm else out


def moe_combine(x, y, pos, routed, mod3, gate_chunk, mod_row, final_g=None):
    t, d = x.shape
    fg = jnp.ones((1, d), F32) if final_g is None else final_g.reshape(1, d)
    return pl.pallas_call(
        functools.partial(_combine_kernel, final_norm=final_g is not None),
        out_shape=jax.ShapeDtypeStruct((t, d), F32),
        grid_spec=pltpu.PrefetchScalarGridSpec(
            num_scalar_prefetch=2,
            grid=(t // ROW_TILE,),
            in_specs=[
                pl.BlockSpec((ROW_TILE, d), lambda i, a, b: (i, 0)),
                pl.BlockSpec((1, 1, d), lambda i, a, b: (mod_row(i), 0, gate_chunk)),
                pl.BlockSpec((ROW_TILE, LANES), lambda i, a, b: (i, 0)),
                pl.BlockSpec((1, d), lambda i, a, b: (0, 0)),
                pl.BlockSpec(memory_space=pl.ANY),
            ],
            out_specs=pl.BlockSpec((ROW_TILE, d), lambda i, a, b: (i, 0)),
            scratch_shapes=[pltpu.VMEM((2, 2, ROW_TILE, d), F32), pltpu.SemaphoreType.DMA((2,))],
        ),
        compiler_params=_cparams(("arbitrary",)),
        name="moe_combine",
    )(pos[:, 0], pos[:, 1], x, mod3, routed, fg, y)


def kernel(x, c, ctx, c_ctx, ada_w, ada_b, norm1_g, w_in, gla_wa_fwd, gla_ba_fwd, gla_wa_bwd, gla_ba_bwd, gla_norm_g,
           gqa_qnorm_g, gqa_knorm_g, diff_lq1, diff_lk1, diff_lq2, diff_lk2, diff_norm_g, w_out, norm2_g, router_wg,
           router_bg, router_we, router_be, moe_w_gate, moe_w_up, moe_w_down, final_norm_g):
    nb, seq, d = x.shape
    n_ctx = ctx.shape[1]
    depth = ada_w.shape[0]
    assert n_ctx == ROW_TILE and seq % ROW_TILE == 0 and seq % GRID_W == 0
    rpb = n_ctx + seq
    tpb = rpb // ROW_TILE
    t = nb * rpb

    gla_heads = (d // 4) // GLA_DV
    gqa_heads = (d // 2) // HEAD_DIM
    n_kv = gqa_heads // GQA_GROUP
    diff_heads = (d // 4) // DIFF_DV
    kw = gla_heads * GLA_DK
    vw = gla_heads * GLA_DV
    splits = (kw, kw, vw, vw, GLA_RANK, GLA_RANK, gqa_heads * HEAD_DIM, n_kv * HEAD_DIM, n_kv * HEAD_DIM,
              diff_heads * 2 * HEAD_DIM, diff_heads * 2 * HEAD_DIM, diff_heads * DIFF_DV)
    offs = [0]
    for s in splits:
        offs.append(offs[-1] + s)
    col = {}
    acc = 0
    for name, width in (("aq", kw), ("ak", kw), ("av", vw), ("ag", vw), ("bq", splits[6]), ("bk", splits[7]),
                        ("bv", splits[8]), ("dq", splits[9]), ("dk", splits[10]), ("dv", splits[11])):
        col[name] = acc
        acc += width

    def mod_row_full(i):
        return jnp.where(i % tpb == 0, nb, i // tpb)

    def mod_row_latent(i):
        return i // (tpb - 1)

    tokens = jnp.concatenate([ctx, x], axis=1).reshape(t, d)
    c_all = jnp.zeros((8, d), F32).at[:nb].set(c).at[nb].set(c_ctx)
    mod = ada_modulation(c_all, ada_w, ada_b)
    cos, sin = rope_tables(n_ctx, seq)
    w_in_t = jnp.swapaxes(w_in, 1, 2)
    w_wide = drop_rows_transpose_bf16(w_in_t, offs[4], offs[6])
    w_out_bf = w_out.astype(BF16)
    f = moe_w_gate.shape[-1]
    wg_all = moe_w_gate.reshape(depth * N_EXPERTS, d, f)
    wu_all = moe_w_up.reshape(depth * N_EXPERTS, d, f)
    wd_all = moe_w_down.reshape(depth * N_EXPERTS, f, d)

    for l in range(depth):
        last = l == depth - 1
        lam_init = 0.8 - 0.6 * math.exp(-0.3 * l)
        mod3 = mod[l].reshape(8, 1, 6 * d)

        w_dec = jnp.zeros((d, LANES), F32).at[:, :2 * GLA_RANK].set(w_in_t[l, offs[4]:offs[6], :].T)
        h, paa = norm_modulate(tokens, norm1_g[l], mod3, 1, 0, w_dec, mod_row_full, h_dtype=BF16, exact_small=False)
        p = matmul(h, w_wide, l, bm=1024, bn=1024, out_dtype=BF16)

        wa_blk = jnp.zeros((2, LANES, kw), F32)
        wa_blk = wa_blk.at[0, :GLA_RANK].set(gla_wa_fwd[l]).at[1, GLA_RANK:2 * GLA_RANK].set(gla_wa_bwd[l])
        ba_blk = jnp.stack([gla_ba_fwd[l], gla_ba_bwd[l]]).reshape(2, 1, kw)
        o_f, o_b = gla_scan(p, paa, wa_blk, ba_blk, nb=nb, rows_per_batch=rpb, n_ctx_chunks=n_ctx // GLA_CHUNK,
                            nh=gla_heads)
        if last:
            x_tile, mod_row = _LatentTiles(nb, tpb), mod_row_latent
        else:
            x_tile, mod_row = _AllTiles(t // ROW_TILE), mod_row_full
        mix_a = gla_output(o_f, o_b, p, gla_norm_g[l], x_tile, gate_block=col["ag"] // vw, nh=gla_heads)
        mix_b = gqa_attention(p, cos, sin, gqa_qnorm_g[l], gqa_knorm_g[l], nb=nb, rows_per_batch=rpb, n_ctx=n_ctx,
                              n_kv=n_kv, q_col=col["bq"], k_col=col["bk"], v_col=col["bv"], with_ctx=not last)
        mix_c = diff_attention(p, cos, sin, diff_lq1[l], diff_lk1[l], diff_lq2[l], diff_lk2[l], diff_norm_g[l],
                               nb=nb, rows_per_batch=rpb, n_ctx=n_ctx, nh=diff_heads, q_col=col["dq"],
                               k_col=col["dk"], v_col=col["dv"], with_ctx=not last, lam_init=lam_init)
        tokens = matmul3_gated_residual(mix_a, mix_b, mix_c, w_out_bf, l, tokens, mod3, 2, x_tile, mod_row, bn=1024)

        w_route = jnp.zeros((d, LANES), F32).at[:, :N_GROUPS].set(router_wg[l])
        w_route = w_route.at[:, N_GROUPS:N_GROUPS + N_EXPERTS].set(router_we[l])
        b_route = jnp.zeros((1, LANES), F32).at[0, :N_GROUPS].set(router_bg[l])
        b_route = b_route.at[0, N_GROUPS:N_GROUPS + N_EXPERTS].set(router_be[l])
        h2, logits = norm_modulate(tokens, norm2_g[l], mod3, 4, 3, w_route, mod_row, h_dtype=F32, exact_small=True)
        routed = route(logits, b_route)
        plan = dispatch_plan(routed[:, 0:2].astype(jnp.int32))
        y = expert_mlp(h2, plan, wg_all, wu_all, wd_all, expert_base=l * N_EXPERTS)
        tokens = moe_combine(tokens, y, plan[1], routed, mod3, 5, mod_row, final_norm_g if last else None)

    return tokens.reshape(nb, seq, d)
```

```python
import functools
import math

import jax
import jax.numpy as jnp
from jax import lax
from jax.experimental import pallas as pl
from jax.experimental.pallas import tpu as pltpu

F32 = jnp.float32
BF16 = jnp.bfloat16

HEAD_DIM = 128
GRID_W = 64
ROPE_THETA = 10000.0
EPS = 1e-6
GLA_DK = HEAD_DIM
GLA_DV = 2 * HEAD_DIM
GLA_RANK = 16
GLA_TAU = 16.0
GLA_CHUNK = 64
GQA_GROUP = 4
GQA_KV_PER_STEP = 2
DIFF_DV = 2 * HEAD_DIM
N_GROUPS = 4
EXPERTS_PER_GROUP = 8
N_EXPERTS = N_GROUPS * EXPERTS_PER_GROUP

LANES = 128
ROW_TILE = 256
SEG_TILE = 128
SEG_ROWS = 768
W_CHUNKS = 8
W_SLOTS = 3
VMEM_LIMIT = 56 * 1024 * 1024


def _cparams(sem, vmem=VMEM_LIMIT):
    return pltpu.CompilerParams(dimension_semantics=sem, vmem_limit_bytes=vmem)


def _sigmoid(x):
    return 1.0 / (1.0 + jnp.exp(-x))


def _silu(x):
    return x * _sigmoid(x)


def _log_sigmoid(x):
    return jnp.minimum(x, 0.0) - jnp.log(1.0 + jnp.exp(-jnp.abs(x)))


def _rms(x, g):
    return x * lax.rsqrt(jnp.mean(x * x, axis=-1, keepdims=True) + EPS) * g


def _dot(a, b):
    return jnp.dot(a, b, preferred_element_type=F32)


def _dot_nt(a, b):
    return lax.dot_general(a, b, (((1,), (1,)), ((), ())), preferred_element_type=F32)


def _dot_tn(a, b):
    return lax.dot_general(a, b, (((0,), (0,)), ((), ())), preferred_element_type=F32)


class _AllTiles:
    def __init__(self, n):
        self.n = n

    def __call__(self, i):
        return i


class _LatentTiles:
    def __init__(self, nb, tiles_per_batch):
        self.n = nb * (tiles_per_batch - 1)
        self._lat = tiles_per_batch - 1
        self._tpb = tiles_per_batch

    def __call__(self, i):
        return (i // self._lat) * self._tpb + 1 + i % self._lat


def _ada_kernel(c_ref, w_ref, b_ref, o_ref):
    s = _silu(c_ref[...]).astype(BF16)
    o_ref[0] = _dot(s, w_ref[0].astype(BF16)) + b_ref[0]


def ada_modulation(c_all, ada_w, ada_b, *, tn=512):
    nl, d, n = ada_w.shape
    return pl.pallas_call(
        _ada_kernel,
        out_shape=jax.ShapeDtypeStruct((nl, 8, n), F32),
        grid=(nl, n // tn),
        in_specs=[
            pl.BlockSpec((8, d), lambda l, j: (0, 0)),
            pl.BlockSpec((1, d, tn), lambda l, j: (l, 0, j)),
            pl.BlockSpec((1, 1, tn), lambda l, j: (l, 0, j)),
        ],
        out_specs=pl.BlockSpec((1, 8, tn), lambda l, j: (l, 0, j)),
        compiler_params=_cparams(("arbitrary", "arbitrary")),
        name="ada_modulation",
    )(c_all, ada_w, ada_b.reshape(nl, 1, n))


def _norm_kernel(x_ref, g_ref, sc_ref, sh_ref, w_ref, h_ref, s_ref, *, exact_small):
    h = _rms(x_ref[...], g_ref[...]) * (1.0 + sc_ref[0]) + sh_ref[0]
    h_ref[...] = h.astype(h_ref.dtype)
    if exact_small:
        w = w_ref[...]
        h_hi, w_hi = h.astype(BF16), w.astype(BF16)
        h_lo, w_lo = (h - h_hi.astype(F32)).astype(BF16), (w - w_hi.astype(F32)).astype(BF16)
        s_ref[...] = _dot(h_hi, w_hi) + _dot(h_hi, w_lo) + _dot(h_lo, w_hi)
    else:
        s_ref[...] = _dot(h.astype(BF16), w_ref[...].astype(BF16))


def norm_modulate(x, g, mod3, sc_chunk, sh_chunk, w_small, mod_row, *, h_dtype, exact_small):
    t, d = x.shape
    ns = w_small.shape[1]
    return pl.pallas_call(
        functools.partial(_norm_kernel, exact_small=exact_small),
        out_shape=(jax.ShapeDtypeStruct((t, d), h_dtype), jax.ShapeDtypeStruct((t, ns), F32)),
        grid=(t // ROW_TILE,),
        in_specs=[
            pl.BlockSpec((ROW_TILE, d), lambda i: (i, 0)),
            pl.BlockSpec((1, d), lambda i: (0, 0)),
            pl.BlockSpec((1, 1, d), lambda i: (mod_row(i), 0, sc_chunk)),
            pl.BlockSpec((1, 1, d), lambda i: (mod_row(i), 0, sh_chunk)),
            pl.BlockSpec((d, ns), lambda i: (0, 0)),
        ],
        out_specs=(pl.BlockSpec((ROW_TILE, d), lambda i: (i, 0)), pl.BlockSpec((ROW_TILE, ns), lambda i: (i, 0))),
        compiler_params=_cparams(("arbitrary",)),
        name="norm_modulate",
    )(x, g.reshape(1, d), mod3, mod3, w_small)


def _split_transpose_kernel(w_ref, cut_ref, o_ref, oc_ref, *, n_cut):
    o_ref[0] = w_ref[0].T.astype(o_ref.dtype)

    @pl.when(pl.program_id(1) == 0)
    def _():
        cut = cut_ref[0].T
        lane = lax.broadcasted_iota(jnp.int32, cut.shape, 1)
        oc_ref[0] = jnp.where(lane < n_cut, cut, 0.0)


def split_rows_transpose(w_t, lo, hi, *, bn=512):
    nl, n, k = w_t.shape
    n_cut = hi - lo
    n_out = n - n_cut
    assert lo % bn == 0 and n_out % bn == 0 and n_cut % 8 == 0 and n_cut <= LANES and lo + LANES <= n

    def src_row(j):
        return pl.multiple_of(j * bn + jnp.where(j * bn >= lo, n_cut, 0), 8)

    def element_rows(rows, row_map):
        return pl.BlockSpec((pl.Element(1), pl.Element(rows), pl.Element(k)), lambda l, j: (l, row_map(j), 0))

    return pl.pallas_call(
        functools.partial(_split_transpose_kernel, n_cut=n_cut),
        out_shape=(jax.ShapeDtypeStruct((nl, k, n_out), BF16), jax.ShapeDtypeStruct((nl, k, LANES), F32)),
        grid=(nl, n_out // bn),
        in_specs=[element_rows(bn, src_row), element_rows(LANES, lambda j: lo)],
        out_specs=(pl.BlockSpec((1, k, bn), lambda l, j: (l, 0, j)), pl.BlockSpec((1, k, LANES), lambda l, j: (l, 0, 0))),
        compiler_params=_cparams(("arbitrary", "arbitrary")),
        name="split_rows_transpose",
    )(w_t, w_t)


def _mm_kernel(a_ref, b_ref, o_ref):
    o_ref[...] = _dot(a_ref[...], b_ref[...]).astype(o_ref.dtype)


def matmul(a, b_all, layer, *, bm, bn, out_dtype):
    m, k = a.shape
    n = b_all.shape[2]
    return pl.pallas_call(
        _mm_kernel,
        out_shape=jax.ShapeDtypeStruct((m, n), out_dtype),
        grid=(m // bm, n // bn),
        in_specs=[pl.BlockSpec((bm, k), lambda i, j: (i, 0)), pl.BlockSpec((None, k, bn), lambda i, j: (layer, 0, j))],
        out_specs=pl.BlockSpec((bm, bn), lambda i, j: (i, j)),
        compiler_params=_cparams(("arbitrary", "arbitrary")),
        name="matmul",
    )(a, b_all)


def _mm3_res_kernel(a1_ref, a2_ref, a3_ref, b_ref, x_ref, g_ref, o_ref):
    k1 = a1_ref.shape[1]
    k2 = k1 + a2_ref.shape[1]
    acc = _dot(a1_ref[...], b_ref[0:k1, :]) + _dot(a2_ref[...], b_ref[k1:k2, :]) + _dot(a3_ref[...], b_ref[k2:, :])
    o_ref[...] = x_ref[...] + g_ref[0] * acc


def matmul3_gated_residual(a1, a2, a3, b_all, layer, x, mod3, gate_chunk, x_tile, mod_row, *, bn):
    k = b_all.shape[1]
    n = b_all.shape[2]
    nbn = n // bn
    return pl.pallas_call(
        _mm3_res_kernel,
        out_shape=jax.ShapeDtypeStruct((x_tile.n * ROW_TILE, n), F32),
        grid=(nbn, x_tile.n),
        in_specs=[
            pl.BlockSpec((ROW_TILE, a1.shape[1]), lambda j, i: (i, 0)),
            pl.BlockSpec((ROW_TILE, a2.shape[1]), lambda j, i: (i, 0)),
            pl.BlockSpec((ROW_TILE, a3.shape[1]), lambda j, i: (i, 0)),
            pl.BlockSpec((None, k, bn), lambda j, i: (layer, 0, j)),
            pl.BlockSpec((ROW_TILE, bn), lambda j, i: (x_tile(i), j)),
            pl.BlockSpec((1, 1, bn), lambda j, i: (mod_row(i), 0, gate_chunk * nbn + j)),
        ],
        out_specs=pl.BlockSpec((ROW_TILE, bn), lambda j, i: (i, j)),
        compiler_params=_cparams(("arbitrary", "arbitrary")),
        name="matmul3_gated_residual",
    )(a1, a2, a3, b_all, x, mod3)


def _gla_direction(q_ref, k_ref, v_ref, paa_ref, wa, ba, o_ref, st_ref, *, nh, backward):
    c = GLA_CHUNK
    row = lax.broadcasted_iota(jnp.int32, (c, c), 0)
    col = lax.broadcasted_iota(jnp.int32, (c, c), 1)
    incl = (col >= row) if backward else (col <= row)
    tri = incl.astype(BF16)

    z = _dot(paa_ref[...].astype(BF16), wa.astype(BF16)) + ba
    la = _log_sigmoid(z) * (1.0 / GLA_TAU)
    la_hi = la.astype(BF16)
    la_lo = (la - la_hi.astype(F32)).astype(BF16)
    b = _dot(tri, la_hi) + _dot(tri, la_lo)
    tot = jnp.sum(la, axis=0, keepdims=True)
    mid = 0.5 * tot
    e_q = jnp.exp(b - mid)
    e_k = jnp.exp(mid - b)
    e_in = jnp.exp(b)
    e_out = jnp.exp(tot - b)
    e_tot = jnp.exp(tot)

    for h in range(nh):
        ks = slice(h * GLA_DK, (h + 1) * GLA_DK)
        vs = slice(h * GLA_DV, (h + 1) * GLA_DV)
        q = q_ref[:, ks].astype(F32) * (GLA_DK ** -0.5)
        k = k_ref[:, ks].astype(F32)
        v = v_ref[:, vs]
        s = _dot_nt((q * e_q[:, ks]).astype(BF16), (k * e_k[:, ks]).astype(BF16))
        s = jnp.where(incl, s, 0.0)
        st = st_ref[h]
        o = _dot(s.astype(BF16), v) + _dot_nt((q * e_in[:, ks]).astype(BF16), st.astype(BF16))
        st_ref[h] = st * e_tot[:, ks] + _dot_tn(v, (k * e_out[:, ks]).astype(BF16))
        o_ref[:, vs] = o


def _gla_kernel(qf, kf, vf, pf, qb, kb, vb, pb, wa_ref, ba_ref, of_ref, ob_ref, st_ref, *, nh):
    @pl.when(pl.program_id(1) == 0)
    def _():
        st_ref[...] = jnp.zeros_like(st_ref)

    _gla_direction(qf, kf, vf, pf, wa_ref[0], ba_ref[0], of_ref, st_ref.at[0], nh=nh, backward=False)
    _gla_direction(qb, kb, vb, pb, wa_ref[1], ba_ref[1], ob_ref, st_ref.at[1], nh=nh, backward=True)


def gla_scan(p, paa, wa_blk, ba_blk, *, nb, rows_per_batch, n_ctx_chunks, nh):
    t = p.shape[0]
    c = GLA_CHUNK
    ncb = rows_per_batch // c
    kw = nh * GLA_DK
    vw = nh * GLA_DV

    def fwd(b, g):
        return b * ncb + g

    def bwd(b, g):
        return b * ncb + jnp.where(g < n_ctx_chunks, n_ctx_chunks - 1 - g, ncb - 1 + n_ctx_chunks - g)

    def chunk_specs(cm):
        return [
            pl.BlockSpec((c, kw), lambda b, g: (cm(b, g), 0)),
            pl.BlockSpec((c, kw), lambda b, g: (cm(b, g), 1)),
            pl.BlockSpec((c, vw), lambda b, g: (cm(b, g), 2 * kw // vw)),
            pl.BlockSpec((c, LANES), lambda b, g: (cm(b, g), 0)),
        ]

    out = jax.ShapeDtypeStruct((t, vw), F32)
    return pl.pallas_call(
        functools.partial(_gla_kernel, nh=nh),
        out_shape=(out, out),
        grid=(nb, ncb),
        in_specs=chunk_specs(fwd) + chunk_specs(bwd) + [
            pl.BlockSpec((2, LANES, kw), lambda b, g: (0, 0, 0)),
            pl.BlockSpec((2, 1, kw), lambda b, g: (0, 0, 0)),
        ],
        out_specs=(pl.BlockSpec((c, vw), lambda b, g: (fwd(b, g), 0)), pl.BlockSpec((c, vw), lambda b, g: (bwd(b, g), 0))),
        scratch_shapes=[pltpu.VMEM((2, nh, GLA_DV, GLA_DK), F32)],
        compiler_params=_cparams(("arbitrary", "arbitrary")),
        name="gla_scan",
    )(p, p, p, paa, p, p, p, paa, wa_blk, ba_blk)


def _gla_out_kernel(of_ref, ob_ref, gate_ref, gn_ref, out_ref, *, nh):
    o = of_ref[...] + ob_ref[...]
    for h in range(nh):
        vs = slice(h * GLA_DV, (h + 1) * GLA_DV)
        gate = gate_ref[:, vs].astype(F32)
        out_ref[:, vs] = (_rms(o[:, vs], gn_ref[...]) * _silu(gate)).astype(out_ref.dtype)


def gla_output(o_f, o_b, p, gn, in_tile, *, gate_block, nh):
    vw = nh * GLA_DV
    row = pl.BlockSpec((ROW_TILE, vw), lambda i: (in_tile(i), 0))
    return pl.pallas_call(
        functools.partial(_gla_out_kernel, nh=nh),
        out_shape=jax.ShapeDtypeStruct((in_tile.n * ROW_TILE, vw), BF16),
        grid=(in_tile.n,),
        in_specs=[row, row, pl.BlockSpec((ROW_TILE, vw), lambda i: (in_tile(i), gate_block)),
                  pl.BlockSpec((1, GLA_DV), lambda i: (0, 0))],
        out_specs=pl.BlockSpec((ROW_TILE, vw), lambda i: (i, 0)),
        compiler_params=_cparams(("arbitrary",)),
        name="gla_output",
    )(o_f, o_b, p, gn.reshape(1, GLA_DV))


def _rope(x, cos, sin_signed):
    lane = lax.broadcasted_iota(jnp.int32, x.shape, 1)
    first = (lane % (HEAD_DIM // 2)) < (HEAD_DIM // 4)
    rot = jnp.where(first, pltpu.roll(x, HEAD_DIM - HEAD_DIM // 4, 1), pltpu.roll(x, HEAD_DIM // 4, 1))
    return x * cos + rot * sin_signed


def rope_tables(n_ctx, seq):
    nf = HEAD_DIM // 4
    rows = seq // GRID_W
    row = jnp.repeat(jnp.arange(rows, dtype=jnp.int32), GRID_W).astype(F32)
    col = jnp.tile(jnp.arange(GRID_W, dtype=jnp.int32), rows).astype(F32)
    inv = ROPE_THETA ** (-jnp.arange(nf, dtype=F32) / nf)
    ang = jnp.concatenate([row[:, None] * inv, row[:, None] * inv, col[:, None] * inv, col[:, None] * inv], axis=1)
    sign = jnp.tile(jnp.concatenate([-jnp.ones((nf,), F32), jnp.ones((nf,), F32)]), 2)
    cos = jnp.concatenate([jnp.ones((n_ctx, HEAD_DIM), F32), jnp.cos(ang)], axis=0)
    sin = jnp.concatenate([jnp.zeros((n_ctx, HEAD_DIM), F32), jnp.sin(ang) * sign], axis=0)
    return cos, sin


def _attend(q, keys, vals):
    s = _dot_nt(q, keys)
    e = jnp.exp(s - jnp.max(s, axis=-1, keepdims=True))
    return _dot(e.astype(BF16), vals), jnp.sum(e, axis=-1, keepdims=True)


def _gqa_kernel(q_ref, k_ref, v_ref, cq_ref, sq_ref, ck_ref, sk_ref, qn_ref, kn_ref, o_ref, ks_ref, *,
                n_ctx, ctx_tile_first):
    qi = pl.program_id(2)

    n_kv = k_ref.shape[1] // HEAD_DIM

    @pl.when(qi == 0)
    def _():
        for j in range(n_kv):
            cs = slice(j * HEAD_DIM, (j + 1) * HEAD_DIM)
            k = _rms(k_ref[:, cs].astype(F32), kn_ref[...])
            ks_ref[:, cs] = _rope(k, ck_ref[...], sk_ref[...]).astype(BF16)

    def attend(nk):
        for j in range(n_kv):
            cs = slice(j * HEAD_DIM, (j + 1) * HEAD_DIM)
            vals = v_ref[0:nk, cs]
            for g in range(GQA_GROUP):
                head = j * GQA_GROUP + g
                hs = slice(head * HEAD_DIM, (head + 1) * HEAD_DIM)
                q = _rms(q_ref[:, hs].astype(F32), qn_ref[...])
                q = (_rope(q, cq_ref[...], sq_ref[...]) * (HEAD_DIM ** -0.5)).astype(BF16)
                pv, l = _attend(q, ks_ref[0:nk, cs], vals)
                o_ref[:, hs] = (pv / l).astype(o_ref.dtype)

    if ctx_tile_first:
        @pl.when(qi == 0)
        def _():
            attend(n_ctx)

        @pl.when(qi > 0)
        def _():
            attend(k_ref.shape[0])
    else:
        attend(k_ref.shape[0])


def gqa_attention(p, cos, sin, qn, kn, *, nb, rows_per_batch, n_ctx, n_kv, q_col, k_col, v_col, with_ctx):
    t = p.shape[0]
    tpb = rows_per_batch // ROW_TILE
    nq = tpb if with_ctx else tpb - 1
    off = 0 if with_ctx else 1
    kvw = GQA_KV_PER_STEP * HEAD_DIM
    qw = GQA_GROUP * kvw
    return pl.pallas_call(
        functools.partial(_gqa_kernel, n_ctx=n_ctx, ctx_tile_first=with_ctx),
        out_shape=jax.ShapeDtypeStruct((nb * nq * ROW_TILE, n_kv * GQA_GROUP * HEAD_DIM), BF16),
        grid=(nb, n_kv // GQA_KV_PER_STEP, nq),
        in_specs=[
            pl.BlockSpec((ROW_TILE, qw), lambda b, h, i: (b * tpb + i + off, q_col // qw + h)),
            pl.BlockSpec((rows_per_batch, kvw), lambda b, h, i: (b, k_col // kvw + h)),
            pl.BlockSpec((rows_per_batch, kvw), lambda b, h, i: (b, v_col // kvw + h)),
            pl.BlockSpec((ROW_TILE, HEAD_DIM), lambda b, h, i: (i + off, 0)),
            pl.BlockSpec((ROW_TILE, HEAD_DIM), lambda b, h, i: (i + off, 0)),
            pl.BlockSpec((rows_per_batch, HEAD_DIM), lambda b, h, i: (0, 0)),
            pl.BlockSpec((rows_per_batch, HEAD_DIM), lambda b, h, i: (0, 0)),
            pl.BlockSpec((1, HEAD_DIM), lambda b, h, i: (0, 0)),
            pl.BlockSpec((1, HEAD_DIM), lambda b, h, i: (0, 0)),
        ],
        out_specs=pl.BlockSpec((ROW_TILE, qw), lambda b, h, i: (b * nq + i, h)),
        scratch_shapes=[pltpu.VMEM((rows_per_batch, kvw), BF16)],
        compiler_params=_cparams(("arbitrary", "arbitrary", "arbitrary")),
        name="gqa_attention",
    )(p, p, p, cos, sin, cos, sin, qn.reshape(1, HEAD_DIM), kn.reshape(1, HEAD_DIM))


def _diff_kernel(q_ref, k_ref, v_ref, cq_ref, sq_ref, ck_ref, sk_ref, lq1_ref, lk1_ref, lq2_ref, lk2_ref, gn_ref,
                 o_ref, ks_ref, *, n_ctx, ctx_tile_first, lam_init):
    qi = pl.program_id(1)
    nh = v_ref.shape[1] // DIFF_DV

    @pl.when(qi == 0)
    def _():
        for j in range(2 * nh):
            hs = slice(j * HEAD_DIM, (j + 1) * HEAD_DIM)
            ks_ref[:, hs] = _rope(k_ref[:, hs].astype(F32), ck_ref[...], sk_ref[...]).astype(BF16)

    lam = (jnp.exp(jnp.sum(lq1_ref[...] * lk1_ref[...], axis=-1, keepdims=True))
           - jnp.exp(jnp.sum(lq2_ref[...] * lk2_ref[...], axis=-1, keepdims=True)) + lam_init)

    def attend(nk):
        for h in range(nh):
            vs = slice(h * DIFF_DV, (h + 1) * DIFF_DV)
            vals = v_ref[0:nk, vs]
            o = None
            for j in range(2):
                hs = slice((2 * h + j) * HEAD_DIM, (2 * h + j + 1) * HEAD_DIM)
                q = (_rope(q_ref[:, hs].astype(F32), cq_ref[...], sq_ref[...]) * (HEAD_DIM ** -0.5)).astype(BF16)
                pv, l = _attend(q, ks_ref[0:nk, hs], vals)
                o = pv / l if j == 0 else o - (lam / l) * pv
            o_ref[:, vs] = (_rms(o, gn_ref[...]) * (1.0 - lam_init)).astype(o_ref.dtype)

    if ctx_tile_first:
        @pl.when(qi == 0)
        def _():
            attend(n_ctx)

        @pl.when(qi > 0)
        def _():
            attend(k_ref.shape[0])
    else:
        attend(k_ref.shape[0])


def diff_attention(p, cos, sin, lq1, lk1, lq2, lk2, gn, *, nb, rows_per_batch, n_ctx, nh, q_col, k_col, v_col,
                   with_ctx, lam_init):
    t = p.shape[0]
    tpb = rows_per_batch // ROW_TILE
    nq = tpb if with_ctx else tpb - 1
    off = 0 if with_ctx else 1
    w = nh * DIFF_DV
    vec = pl.BlockSpec((1, HEAD_DIM), lambda b, i: (0, 0))
    return pl.pallas_call(
        functools.partial(_diff_kernel, n_ctx=n_ctx, ctx_tile_first=with_ctx, lam_init=lam_init),
        out_shape=jax.ShapeDtypeStruct((nb * nq * ROW_TILE, w), BF16),
        grid=(nb, nq),
        in_specs=[
            pl.BlockSpec((ROW_TILE, w), lambda b, i: (b * tpb + i + off, q_col // w)),
            pl.BlockSpec((rows_per_batch, w), lambda b, i: (b, k_col // w)),
            pl.BlockSpec((rows_per_batch, w), lambda b, i: (b, v_col // w)),
            pl.BlockSpec((ROW_TILE, HEAD_DIM), lambda b, i: (i + off, 0)),
            pl.BlockSpec((ROW_TILE, HEAD_DIM), lambda b, i: (i + off, 0)),
            pl.BlockSpec((rows_per_batch, HEAD_DIM), lambda b, i: (0, 0)),
            pl.BlockSpec((rows_per_batch, HEAD_DIM), lambda b, i: (0, 0)),
            vec, vec, vec, vec,
            pl.BlockSpec((1, DIFF_DV), lambda b, i: (0, 0)),
        ],
        out_specs=pl.BlockSpec((ROW_TILE, w), lambda b, i: (b * nq + i, 0)),
        scratch_shapes=[pltpu.VMEM((rows_per_batch, w), BF16)],
        compiler_params=_cparams(("arbitrary", "arbitrary")),
        name="diff_attention",
    )(p, p, p, cos, sin, cos, sin, lq1.reshape(1, -1), lk1.reshape(1, -1), lq2.reshape(1, -1), lk2.reshape(1, -1),
      gn.reshape(1, DIFF_DV))


def _route_kernel(lg_ref, bias_ref, o_ref):
    x = lg_ref[...] + bias_ref[...]
    lane = lax.broadcasted_iota(jnp.int32, x.shape, 1).astype(F32)
    neg = -jnp.inf

    def first_max(vals, mask):
        v = jnp.where(mask, vals, neg)
        m = jnp.max(v, axis=-1, keepdims=True)
        idx = jnp.min(jnp.where(mask & (v == m), lane, float(LANES)), axis=-1, keepdims=True)
        return m, idx

    gmask = lane < N_GROUPS
    gm, gidx = first_max(x, gmask)
    g_w = 1.0 / jnp.sum(jnp.where(gmask, jnp.exp(x - gm), 0.0), axis=-1, keepdims=True)
    lo = N_GROUPS + EXPERTS_PER_GROUP * gidx
    emask = (lane >= lo) & (lane < lo + EXPERTS_PER_GROUP)
    m1, i1 = first_max(x, emask)
    m2, i2 = first_max(x, emask & (lane != i1))
    r = jnp.exp(m2 - m1)
    w1 = g_w / (1.0 + r)
    w2 = g_w * r / (1.0 + r)
    out = jnp.where(lane == 0, i1 - N_GROUPS, 0.0)
    out = jnp.where(lane == 1, i2 - N_GROUPS, out)
    out = jnp.where(lane == 2, w1, out)
    out = jnp.where(lane == 3, w2, out)
    o_ref[...] = out


def route(logits, bias):
    t = logits.shape[0]
    return pl.pallas_call(
        _route_kernel,
        out_shape=jax.ShapeDtypeStruct((t, LANES), F32),
        grid=(t // ROW_TILE,),
        in_specs=[pl.BlockSpec((ROW_TILE, LANES), lambda i: (i, 0)), pl.BlockSpec((1, LANES), lambda i: (0, 0))],
        out_specs=pl.BlockSpec((ROW_TILE, LANES), lambda i: (i, 0)),
        compiler_params=_cparams(("arbitrary",)),
        name="route",
    )(logits, bias)


def dispatch_plan(ids):
    t = ids.shape[0]
    tps = SEG_ROWS // SEG_TILE
    n_src = 2 * t + N_EXPERTS * SEG_TILE
    n_seg = (n_src // SEG_TILE + (tps - 1) * N_EXPERTS) // tps + 1 + (N_EXPERTS * SEG_TILE) // SEG_ROWS + 2
    e = ids.reshape(-1)
    onehot = (e[:, None] == jnp.arange(N_EXPERTS, dtype=jnp.int32)[None, :]).astype(jnp.int32)
    csum = jnp.cumsum(onehot, axis=0)
    counts = csum[-1]
    tiles = (counts + SEG_TILE - 1) // SEG_TILE
    segs = (tiles + tps - 1) // tps
    seg_end = jnp.cumsum(segs)
    seg_start = seg_end - segs
    row_start = (jnp.cumsum(tiles) - tiles) * SEG_TILE
    rank = csum - 1
    cpos = jnp.sum(onehot * (rank + row_start[None, :]), axis=1)
    src = jnp.zeros((n_src,), jnp.int32).at[cpos].set(jnp.arange(2 * t, dtype=jnp.int32) // 2)
    sidx = jnp.arange(n_seg, dtype=jnp.int32)
    seg_expert = jnp.minimum(jnp.sum((seg_end[None, :] <= sidx[:, None]).astype(jnp.int32), axis=1), N_EXPERTS - 1)
    sel = (seg_expert[:, None] == jnp.arange(N_EXPERTS, dtype=jnp.int32)[None, :]).astype(jnp.int32)
    k = sidx - jnp.sum(sel * seg_start[None, :], axis=1)
    seg_row0 = jnp.sum(sel * row_start[None, :], axis=1) + k * SEG_ROWS
    used = sidx < seg_end[-1]
    seg_nt = jnp.where(used, jnp.clip(jnp.sum(sel * tiles[None, :], axis=1) - k * tps, 0, tps), 0)
    y_rows = n_src + SEG_ROWS
    fill0 = jnp.max(jnp.where(used, seg_row0 + SEG_ROWS, 0)) + (sidx - seg_end[-1]) * SEG_ROWS
    seg_row0 = jnp.where(used, seg_row0, jnp.minimum(fill0, y_rows - SEG_ROWS))
    seg_fill = jnp.where(used, 0, (fill0 < y_rows).astype(jnp.int32))
    return src, cpos.reshape(t, 2), seg_expert, seg_row0, seg_nt, seg_fill, seg_end[-1:]


def _row_copy(src_hbm, idx, dst_vmem, r, sem):
    return pltpu.make_async_copy(src_hbm.at[pl.ds(idx, 1)], dst_vmem.at[pl.ds(r, 1)], sem)


def _expert_kernel(se_ref, r0_ref, nt_ref, fill_ref, nseg_ref, src_ref, h_hbm, wg_hbm, wu_hbm, wd_hbm, y_hbm,
                   land, xs, acc_a, acc_u, hid, st_g, st_u, st_d, wb_g, wb_u, wb_d, yst,
                   sem_x, sem_g, sem_u, sem_d, sem_y, *, expert_base):
    s = pl.program_id(0)
    n_seg = nseg_ref[0]
    d = xs.shape[1]
    ck = st_g.shape[1]
    cn = st_d.shape[2]
    n_k = d // ck
    n_c = d // cn
    assert n_k > W_SLOTS and n_c > W_SLOTS and n_c >= 3

    def start_gather(seg):
        r0 = r0_ref[seg]

        def tile(tt, carry):
            base = tt * SEG_TILE

            def body(r, c):
                _row_copy(h_hbm, src_ref[r0 + base + r], land, base + r, sem_x).start()
                return c

            return lax.fori_loop(0, SEG_TILE, body, carry, unroll=8)

        lax.fori_loop(0, nt_ref[seg], tile, 0)

    def finish_gather(seg):
        nt = nt_ref[seg]

        def wait(t, carry):
            pltpu.make_async_copy(h_hbm.at[pl.ds(0, SEG_TILE)], land.at[pl.ds(0, SEG_TILE)], sem_x).wait()
            return carry

        lax.fori_loop(0, nt, wait, 0)

        def convert(t, carry):
            rows = pl.ds(pl.multiple_of(t * SEG_TILE, SEG_TILE), SEG_TILE)
            xs[rows, :] = land[rows, :].astype(BF16)
            return carry

        lax.fori_loop(0, nt, convert, 0)

    def in_copies(e, kc):
        slot = kc % W_SLOTS
        rows = pl.ds(pl.multiple_of(kc * ck, ck), ck)
        return (pltpu.make_async_copy(wg_hbm.at[e, rows], st_g.at[slot], sem_g.at[slot]),
                pltpu.make_async_copy(wu_hbm.at[e, rows], st_u.at[slot], sem_u.at[slot]))

    def out_copy(e, c):
        slot = c % W_SLOTS
        cols = pl.ds(pl.multiple_of(c * cn, cn), cn)
        return pltpu.make_async_copy(wd_hbm.at[e, :, cols], st_d.at[slot], sem_d.at[slot])

    def y_copy(c, slot=None):
        slot = c % 2 if slot is None else slot
        cols = pl.ds(pl.multiple_of(c * cn, cn), cn)
        rows = pl.ds(pl.multiple_of(r0_ref[s], SEG_TILE), SEG_ROWS)
        return pltpu.make_async_copy(yst.at[slot], y_hbm.at[rows, cols], sem_y.at[slot])

    def land_in(e, kc):
        for cp in in_copies(e, kc):
            cp.wait()
        wb_g[kc % 2] = st_g[kc % W_SLOTS].astype(BF16)
        wb_u[kc % 2] = st_u[kc % W_SLOTS].astype(BF16)

        @pl.when(kc + W_SLOTS < n_k)
        def _():
            for cp in in_copies(e, kc + W_SLOTS):
                cp.start()

    def land_out(e, c):
        out_copy(e, c).wait()
        wb_d[c % 2] = st_d[c % W_SLOTS].astype(BF16)

        @pl.when(c + W_SLOTS < n_c)
        def _():
            out_copy(e, c + W_SLOTS).start()

    def gate_up(kc):
        x_k = xs[:, pl.ds(pl.multiple_of(kc * ck, ck), ck)]
        acc_a[...] += _dot(x_k, wb_g[kc % 2])
        acc_u[...] += _dot(x_k, wb_u[kc % 2])

    def down(c):
        yst[c % 2] = _dot(hid[...], wb_d[c % 2])
        y_copy(c).start()

    @pl.when(s == 0)
    def _():
        xs[...] = jnp.zeros_like(xs)
        start_gather(0)

    def start_first_in(seg):
        for i in range(W_SLOTS):
            for cp in in_copies(expert_base + se_ref[seg], i):
                cp.start()

    def start_first_out(seg):
        for i in range(W_SLOTS):
            out_copy(expert_base + se_ref[seg], i).start()

    @pl.when(s == 0)
    def _():
        start_first_in(0)
        start_first_out(0)

    @pl.when(s < n_seg)
    def _():
        e = expert_base + se_ref[s]
        finish_gather(s)

        @pl.when(s + 1 < n_seg)
        def _():
            start_gather(s + 1)

        acc_a[...] = jnp.zeros_like(acc_a)
        acc_u[...] = jnp.zeros_like(acc_u)
        land_in(e, 0)

        def in_step(kc, carry):
            land_in(e, kc + 1)
            gate_up(kc)
            return carry

        lax.fori_loop(0, n_k - 1, in_step, 0)

        @pl.when(s + 1 < n_seg)
        def _():
            start_first_in(s + 1)

        land_out(e, 0)
        gate_up(n_k - 1)
        hid[...] = (_silu(acc_a[...]) * acc_u[...]).astype(BF16)

        def out_step(c, carry):
            @pl.when(c >= 2)
            def _():
                y_copy(c - 2).wait()

            land_out(e, c + 1)
            down(c)
            return carry

        lax.fori_loop(0, n_c - 1, out_step, 0)

        @pl.when(s + 1 < n_seg)
        def _():
            start_first_out(s + 1)

        y_copy(n_c - 3).wait()
        down(n_c - 1)
        y_copy(n_c - 2).wait()
        y_copy(n_c - 1).wait()

    @pl.when(fill_ref[s] == 1)
    def _():
        yst[0] = jnp.zeros_like(yst[0])
        for c in range(n_c):
            y_copy(c, 0).start()
        for c in range(n_c):
            y_copy(c, 0).wait()


def expert_mlp(h, plan, wg, wu, wd, *, expert_base):
    src, _, seg_expert, seg_row0, seg_nt, seg_fill, n_seg = plan
    d = h.shape[1]
    f = wg.shape[2]
    n_grid = seg_expert.shape[0]
    ck = d // W_CHUNKS
    cn = d // W_CHUNKS
    return pl.pallas_call(
        functools.partial(_expert_kernel, expert_base=expert_base),
        out_shape=jax.ShapeDtypeStruct((src.shape[0] + SEG_ROWS, d), F32),
        grid_spec=pltpu.PrefetchScalarGridSpec(
            num_scalar_prefetch=6,
            grid=(n_grid,),
            in_specs=[pl.BlockSpec(memory_space=pl.ANY)] * 4,
            out_specs=pl.BlockSpec(memory_space=pl.ANY),
            scratch_shapes=[
                pltpu.VMEM((SEG_ROWS, d), F32), pltpu.VMEM((SEG_ROWS, d), BF16),
                pltpu.VMEM((SEG_ROWS, f), F32), pltpu.VMEM((SEG_ROWS, f), F32), pltpu.VMEM((SEG_ROWS, f), BF16),
                pltpu.VMEM((W_SLOTS, ck, f), F32), pltpu.VMEM((W_SLOTS, ck, f), F32), pltpu.VMEM((W_SLOTS, f, cn), F32),
                pltpu.VMEM((2, ck, f), BF16), pltpu.VMEM((2, ck, f), BF16), pltpu.VMEM((2, f, cn), BF16),
                pltpu.VMEM((2, SEG_ROWS, cn), F32),
                pltpu.SemaphoreType.DMA, pltpu.SemaphoreType.DMA((W_SLOTS,)), pltpu.SemaphoreType.DMA((W_SLOTS,)),
                pltpu.SemaphoreType.DMA((W_SLOTS,)), pltpu.SemaphoreType.DMA((2,)),
            ],
        ),
        compiler_params=_cparams(("arbitrary",)),
        name="moe_experts",
    )(seg_expert, seg_row0, seg_nt, seg_fill, n_seg, src, h, wg, wu, wd)


def _combine_kernel(p1_ref, p2_ref, x_ref, g_ref, r_ref, fg_ref, y_hbm, o_ref, buf_ref, sem, *, final_norm):
    i = pl.program_id(0)
    rows = x_ref.shape[0]
    slot = i % 2

    def start_tile(tile, sl):
        def start(r, carry):
            _row_copy(y_hbm, p1_ref[tile * rows + r], buf_ref.at[sl, 0], r, sem.at[sl]).start()
            _row_copy(y_hbm, p2_ref[tile * rows + r], buf_ref.at[sl, 1], r, sem.at[sl]).start()
            return carry

        lax.fori_loop(0, rows, start, 0, unroll=8)

    @pl.when(i == 0)
    def _():
        start_tile(0, 0)

    @pl.when(i + 1 < pl.num_programs(0))
    def _():
        start_tile(i + 1, 1 - slot)

    pltpu.make_async_copy(y_hbm.at[pl.ds(0, rows)], buf_ref.at[slot, 0], sem.at[slot]).wait()
    pltpu.make_async_copy(y_hbm.at[pl.ds(0, rows)], buf_ref.at[slot, 1], sem.at[slot]).wait()
    w1 = r_ref[:, 2:3]
    w2 = r_ref[:, 3:4]
    out = x_ref[...] + g_ref[0] * (w1 * buf_ref[slot, 0] + w2 * buf_ref[slot, 1])
    o_ref[...] = _rms(out, fg_ref[...]) if final_norm else out


def moe_combine(x, y, pos, routed, mod3, gate_chunk, mod_row, final_g=None):
    t, d = x.shape
    fg = jnp.ones((1, d), F32) if final_g is None else final_g.reshape(1, d)
    return pl.pallas_call(
        functools.partial(_combine_kernel, final_norm=final_g is not None),
        out_shape=jax.ShapeDtypeStruct((t, d), F32),
        grid_spec=pltpu.PrefetchScalarGridSpec(
            num_scalar_prefetch=2,
            grid=(t // ROW_TILE,),
            in_specs=[
                pl.BlockSpec((ROW_TILE, d), lambda i, a, b: (i, 0)),
                pl.BlockSpec((1, 1, d), lambda i, a, b: (mod_row(i), 0, gate_chunk)),
                pl.BlockSpec((ROW_TILE, LANES), lambda i, a, b: (i, 0)),
                pl.BlockSpec((1, d), lambda i, a, b: (0, 0)),
                pl.BlockSpec(memory_space=pl.ANY),
            ],
            out_specs=pl.BlockSpec((ROW_TILE, d), lambda i, a, b: (i, 0)),
            scratch_shapes=[pltpu.VMEM((2, 2, ROW_TILE, d), F32), pltpu.SemaphoreType.DMA((2,))],
        ),
        compiler_params=_cparams(("arbitrary",)),
        name="moe_combine",
    )(pos[:, 0], pos[:, 1], x, mod3, routed, fg, y)


def kernel(x, c, ctx, c_ctx, ada_w, ada_b, norm1_g, w_in, gla_wa_fwd, gla_ba_fwd, gla_wa_bwd, gla_ba_bwd, gla_norm_g,
           gqa_qnorm_g, gqa_knorm_g, diff_lq1, diff_lk1, diff_lq2, diff_lk2, diff_norm_g, w_out, norm2_g, router_wg,
           router_bg, router_we, router_be, moe_w_gate, moe_w_up, moe_w_down, final_norm_g):
    nb, seq, d = x.shape
    n_ctx = ctx.shape[1]
    depth = ada_w.shape[0]
    assert n_ctx == ROW_TILE and seq % ROW_TILE == 0 and seq % GRID_W == 0
    rpb = n_ctx + seq
    tpb = rpb // ROW_TILE
    t = nb * rpb

    gla_heads = (d // 4) // GLA_DV
    gqa_heads = (d // 2) // HEAD_DIM
    n_kv = gqa_heads // GQA_GROUP
    diff_heads = (d // 4) // DIFF_DV
    kw = gla_heads * GLA_DK
    vw = gla_heads * GLA_DV
    splits = (kw, kw, vw, vw, GLA_RANK, GLA_RANK, gqa_heads * HEAD_DIM, n_kv * HEAD_DIM, n_kv * HEAD_DIM,
              diff_heads * 2 * HEAD_DIM, diff_heads * 2 * HEAD_DIM, diff_heads * DIFF_DV)
    offs = [0]
    for s in splits:
        offs.append(offs[-1] + s)
    col = {}
    acc = 0
    for name, width in (("aq", kw), ("ak", kw), ("av", vw), ("ag", vw), ("bq", splits[6]), ("bk", splits[7]),
                        ("bv", splits[8]), ("dq", splits[9]), ("dk", splits[10]), ("dv", splits[11])):
        col[name] = acc
        acc += width

    def mod_row_full(i):
        return jnp.where(i % tpb == 0, nb, i // tpb)

    def mod_row_latent(i):
        return i // (tpb - 1)

    tokens = jnp.concatenate([ctx, x], axis=1).reshape(t, d)
    c_all = jnp.zeros((8, d), F32).at[:nb].set(c).at[nb].set(c_ctx)
    mod = ada_modulation(c_all, ada_w, ada_b)
    cos, sin = rope_tables(n_ctx, seq)
    w_wide, w_dec_all = split_rows_transpose(jnp.swapaxes(w_in, 1, 2), offs[4], offs[6])
    w_out_bf = w_out.astype(BF16)
    f = moe_w_gate.shape[-1]
    wg_all = moe_w_gate.reshape(depth * N_EXPERTS, d, f)
    wu_all = moe_w_up.reshape(depth * N_EXPERTS, d, f)
    wd_all = moe_w_down.reshape(depth * N_EXPERTS, f, d)

    for l in range(depth):
        last = l == depth - 1
        lam_init = 0.8 - 0.6 * math.exp(-0.3 * l)
        mod3 = mod[l].reshape(8, 1, 6 * d)

        h, paa = norm_modulate(tokens, norm1_g[l], mod3, 1, 0, w_dec_all[l], mod_row_full, h_dtype=BF16,
                               exact_small=False)
        p = matmul(h, w_wide, l, bm=1024, bn=1024, out_dtype=BF16)

        wa_blk = jnp.zeros((2, LANES, kw), F32)
        wa_blk = wa_blk.at[0, :GLA_RANK].set(gla_wa_fwd[l]).at[1, GLA_RANK:2 * GLA_RANK].set(gla_wa_bwd[l])
        ba_blk = jnp.stack([gla_ba_fwd[l], gla_ba_bwd[l]]).reshape(2, 1, kw)
        o_f, o_b = gla_scan(p, paa, wa_blk, ba_blk, nb=nb, rows_per_batch=rpb, n_ctx_chunks=n_ctx // GLA_CHUNK,
                            nh=gla_heads)
        if last:
            x_tile, mod_row = _LatentTiles(nb, tpb), mod_row_latent
        else:
            x_tile, mod_row = _AllTiles(t // ROW_TILE), mod_row_full
        mix_a = gla_output(o_f, o_b, p, gla_norm_g[l], x_tile, gate_block=col["ag"] // vw, nh=gla_heads)
        mix_b = gqa_attention(p, cos, sin, gqa_qnorm_g[l], gqa_knorm_g[l], nb=nb, rows_per_batch=rpb, n_ctx=n_ctx,
                              n_kv=n_kv, q_col=col["bq"], k_col=col["bk"], v_col=col["bv"], with_ctx=not last)
        mix_c = diff_attention(p, cos, sin, diff_lq1[l], diff_lk1[l], diff_lq2[l], diff_lk2[l], diff_norm_g[l],
                               nb=nb, rows_per_batch=rpb, n_ctx=n_ctx, nh=diff_heads, q_col=col["dq"],
                               k_col=col["dk"], v_col=col["dv"], with_ctx=not last, lam_init=lam_init)
        tokens = matmul3_gated_residual(mix_a, mix_b, mix_c, w_out_bf, l, tokens, mod3, 2, x_tile, mod_row, bn=1024)

        w_route = jnp.zeros((d, LANES), F32).at[:, :N_GROUPS].set(router_wg[l])
        w_route = w_route.at[:, N_GROUPS:N_GROUPS + N_EXPERTS].set(router_we[l])
        b_route = jnp.zeros((1, LANES), F32).at[0, :N_GROUPS].set(router_bg[l])
        b_route = b_route.at[0, N_GROUPS:N_GROUPS + N_EXPERTS].set(router_be[l])
        h2, logits = norm_modulate(tokens, norm2_g[l], mod3, 4, 3, w_route, mod_row, h_dtype=F32, exact_small=True)
        routed = route(logits, b_route)
        plan = dispatch_plan(routed[:, 0:2].astype(jnp.int32))
        y = expert_mlp(h2, plan, wg_all, wu_all, wd_all, expert_base=l * N_EXPERTS)
        tokens = moe_combine(tokens, y, plan[1], routed, mod3, 5, mod_row, final_norm_g if last else None)

    return tokens.reshape(nb, seq, d)
```

```python
import functools
import math

import jax
import jax.numpy as jnp
from jax import lax
from jax.experimental import pallas as pl
from jax.experimental.pallas import tpu as pltpu

F32 = jnp.float32
BF16 = jnp.bfloat16

HEAD_DIM = 128
GRID_W = 64
ROPE_THETA = 10000.0
EPS = 1e-6
GLA_DK = HEAD_DIM
GLA_DV = 2 * HEAD_DIM
GLA_RANK = 16
GLA_TAU = 16.0
GLA_CHUNK = 64
GLA_BATCH = 2
GQA_GROUP = 4
GQA_KV_PER_STEP = 2
DIFF_DV = 2 * HEAD_DIM
N_GROUPS = 4
EXPERTS_PER_GROUP = 8
N_EXPERTS = N_GROUPS * EXPERTS_PER_GROUP

LANES = 128
ROW_TILE = 256
SEG_TILE = 128
SEG_ROWS = 768
W_CHUNKS = 8
W_SLOTS = 3
VMEM_LIMIT = 56 * 1024 * 1024


def _cparams(sem, vmem=VMEM_LIMIT):
    return pltpu.CompilerParams(dimension_semantics=sem, vmem_limit_bytes=vmem)


def _sigmoid(x):
    return 1.0 / (1.0 + jnp.exp(-x))


def _silu(x):
    return x * _sigmoid(x)


def _log_sigmoid(x):
    return jnp.minimum(x, 0.0) - jnp.log(1.0 + jnp.exp(-jnp.abs(x)))


def _rms(x, g):
    return x * lax.rsqrt(jnp.mean(x * x, axis=-1, keepdims=True) + EPS) * g


def _dot(a, b):
    return jnp.dot(a, b, preferred_element_type=F32)


def _dot_nt(a, b):
    return lax.dot_general(a, b, (((1,), (1,)), ((), ())), preferred_element_type=F32)


def _dot_tn(a, b):
    return lax.dot_general(a, b, (((0,), (0,)), ((), ())), preferred_element_type=F32)


class _AllTiles:
    def __init__(self, n):
        self.n = n

    def __call__(self, i):
        return i


class _LatentTiles:
    def __init__(self, nb, tiles_per_batch):
        self.n = nb * (tiles_per_batch - 1)
        self._lat = tiles_per_batch - 1
        self._tpb = tiles_per_batch

    def __call__(self, i):
        return (i // self._lat) * self._tpb + 1 + i % self._lat


def _ada_kernel(c_ref, w_ref, b_ref, o_ref):
    s = _silu(c_ref[...]).astype(BF16)
    o_ref[0] = _dot(s, w_ref[0].astype(BF16)) + b_ref[0]


def ada_modulation(c_all, ada_w, ada_b, *, tn=512):
    nl, d, n = ada_w.shape
    return pl.pallas_call(
        _ada_kernel,
        out_shape=jax.ShapeDtypeStruct((nl, 8, n), F32),
        grid=(nl, n // tn),
        in_specs=[
            pl.BlockSpec((8, d), lambda l, j: (0, 0)),
            pl.BlockSpec((1, d, tn), lambda l, j: (l, 0, j)),
            pl.BlockSpec((1, 1, tn), lambda l, j: (l, 0, j)),
        ],
        out_specs=pl.BlockSpec((1, 8, tn), lambda l, j: (l, 0, j)),
        compiler_params=_cparams(("arbitrary", "arbitrary")),
        name="ada_modulation",
    )(c_all, ada_w, ada_b.reshape(nl, 1, n))


def _norm_kernel(x_ref, g_ref, sc_ref, sh_ref, w_ref, h_ref, s_ref, *, exact_small):
    h = _rms(x_ref[...], g_ref[...]) * (1.0 + sc_ref[0]) + sh_ref[0]
    h_ref[...] = h.astype(h_ref.dtype)
    if exact_small:
        w = w_ref[...]
        h_hi, w_hi = h.astype(BF16), w.astype(BF16)
        h_lo, w_lo = (h - h_hi.astype(F32)).astype(BF16), (w - w_hi.astype(F32)).astype(BF16)
        s_ref[...] = _dot(h_hi, w_hi) + _dot(h_hi, w_lo) + _dot(h_lo, w_hi)
    else:
        s_ref[...] = _dot(h.astype(BF16), w_ref[...].astype(BF16))


def norm_modulate(x, g, mod3, sc_chunk, sh_chunk, w_small, mod_row, *, h_dtype, exact_small):
    t, d = x.shape
    ns = w_small.shape[1]
    return pl.pallas_call(
        functools.partial(_norm_kernel, exact_small=exact_small),
        out_shape=(jax.ShapeDtypeStruct((t, d), h_dtype), jax.ShapeDtypeStruct((t, ns), F32)),
        grid=(t // ROW_TILE,),
        in_specs=[
            pl.BlockSpec((ROW_TILE, d), lambda i: (i, 0)),
            pl.BlockSpec((1, d), lambda i: (0, 0)),
            pl.BlockSpec((1, 1, d), lambda i: (mod_row(i), 0, sc_chunk)),
            pl.BlockSpec((1, 1, d), lambda i: (mod_row(i), 0, sh_chunk)),
            pl.BlockSpec((d, ns), lambda i: (0, 0)),
        ],
        out_specs=(pl.BlockSpec((ROW_TILE, d), lambda i: (i, 0)), pl.BlockSpec((ROW_TILE, ns), lambda i: (i, 0))),
        compiler_params=_cparams(("arbitrary",)),
        name="norm_modulate",
    )(x, g.reshape(1, d), mod3, mod3, w_small)


def _split_transpose_kernel(w_ref, cut_ref, o_ref, oc_ref, *, n_cut):
    o_ref[0] = w_ref[0].T.astype(o_ref.dtype)

    @pl.when(pl.program_id(1) == 0)
    def _():
        cut = cut_ref[0].T
        lane = lax.broadcasted_iota(jnp.int32, cut.shape, 1)
        oc_ref[0] = jnp.where(lane < n_cut, cut, 0.0)


def split_rows_transpose(w_t, lo, hi, *, bn=512):
    nl, n, k = w_t.shape
    n_cut = hi - lo
    n_out = n - n_cut
    assert lo % bn == 0 and n_out % bn == 0 and n_cut % 8 == 0 and n_cut <= LANES and lo + LANES <= n

    def src_row(j):
        return pl.multiple_of(j * bn + jnp.where(j * bn >= lo, n_cut, 0), 8)

    def element_rows(rows, row_map):
        return pl.BlockSpec((pl.Element(1), pl.Element(rows), pl.Element(k)), lambda l, j: (l, row_map(j), 0))

    return pl.pallas_call(
        functools.partial(_split_transpose_kernel, n_cut=n_cut),
        out_shape=(jax.ShapeDtypeStruct((nl, k, n_out), BF16), jax.ShapeDtypeStruct((nl, k, LANES), F32)),
        grid=(nl, n_out // bn),
        in_specs=[element_rows(bn, src_row), element_rows(LANES, lambda j: lo)],
        out_specs=(pl.BlockSpec((1, k, bn), lambda l, j: (l, 0, j)), pl.BlockSpec((1, k, LANES), lambda l, j: (l, 0, 0))),
        compiler_params=_cparams(("arbitrary", "arbitrary")),
        name="split_rows_transpose",
    )(w_t, w_t)


def _mm_kernel(a_ref, b_ref, o_ref):
    o_ref[...] = _dot(a_ref[...], b_ref[...]).astype(o_ref.dtype)


def matmul(a, b_all, layer, *, bm, bn, out_dtype):
    m, k = a.shape
    n = b_all.shape[2]
    return pl.pallas_call(
        _mm_kernel,
        out_shape=jax.ShapeDtypeStruct((m, n), out_dtype),
        grid=(m // bm, n // bn),
        in_specs=[pl.BlockSpec((bm, k), lambda i, j: (i, 0)), pl.BlockSpec((None, k, bn), lambda i, j: (layer, 0, j))],
        out_specs=pl.BlockSpec((bm, bn), lambda i, j: (i, j)),
        compiler_params=_cparams(("arbitrary", "arbitrary")),
        name="matmul",
    )(a, b_all)


def _mm3_res_kernel(a1_ref, a2_ref, a3_ref, b_ref, x_ref, g_ref, o_ref):
    k1 = a1_ref.shape[1]
    k2 = k1 + a2_ref.shape[1]
    acc = _dot(a1_ref[...], b_ref[0:k1, :]) + _dot(a2_ref[...], b_ref[k1:k2, :]) + _dot(a3_ref[...], b_ref[k2:, :])
    o_ref[...] = x_ref[...] + g_ref[0] * acc


def matmul3_gated_residual(a1, a2, a3, b_all, layer, x, mod3, gate_chunk, x_tile, mod_row, *, bn):
    k = b_all.shape[1]
    n = b_all.shape[2]
    nbn = n // bn
    return pl.pallas_call(
        _mm3_res_kernel,
        out_shape=jax.ShapeDtypeStruct((x_tile.n * ROW_TILE, n), F32),
        grid=(nbn, x_tile.n),
        in_specs=[
            pl.BlockSpec((ROW_TILE, a1.shape[1]), lambda j, i: (i, 0)),
            pl.BlockSpec((ROW_TILE, a2.shape[1]), lambda j, i: (i, 0)),
            pl.BlockSpec((ROW_TILE, a3.shape[1]), lambda j, i: (i, 0)),
            pl.BlockSpec((None, k, bn), lambda j, i: (layer, 0, j)),
            pl.BlockSpec((ROW_TILE, bn), lambda j, i: (x_tile(i), j)),
            pl.BlockSpec((1, 1, bn), lambda j, i: (mod_row(i), 0, gate_chunk * nbn + j)),
        ],
        out_specs=pl.BlockSpec((ROW_TILE, bn), lambda j, i: (i, j)),
        compiler_params=_cparams(("arbitrary", "arbitrary")),
        name="matmul3_gated_residual",
    )(a1, a2, a3, b_all, x, mod3)


def _gla_direction(q_ref, k_ref, v_ref, paa_ref, wa, ba, o_ref, st_ref, *, nh, backward):
    c = GLA_CHUNK
    row = lax.broadcasted_iota(jnp.int32, (c, c), 0)
    col = lax.broadcasted_iota(jnp.int32, (c, c), 1)
    incl = (col >= row) if backward else (col <= row)
    tri = incl.astype(BF16)

    z = _dot(paa_ref[...].astype(BF16), wa.astype(BF16)) + ba
    la = _log_sigmoid(z) * (1.0 / GLA_TAU)
    la_hi = la.astype(BF16)
    la_lo = (la - la_hi.astype(F32)).astype(BF16)
    b = _dot(tri, la_hi) + _dot(tri, la_lo)
    tot = jnp.sum(la, axis=0, keepdims=True)
    mid = 0.5 * tot
    e_q = jnp.exp(b - mid)
    e_k = jnp.exp(mid - b)
    e_in = jnp.exp(b)
    e_out = jnp.exp(tot - b)
    e_tot = jnp.exp(tot)

    for h in range(nh):
        ks = slice(h * GLA_DK, (h + 1) * GLA_DK)
        vs = slice(h * GLA_DV, (h + 1) * GLA_DV)
        q = q_ref[:, ks].astype(F32) * (GLA_DK ** -0.5)
        k = k_ref[:, ks].astype(F32)
        v = v_ref[:, vs]
        s = _dot_nt((q * e_q[:, ks]).astype(BF16), (k * e_k[:, ks]).astype(BF16))
        s = jnp.where(incl, s, 0.0)
        st = st_ref[h]
        o = _dot(s.astype(BF16), v) + _dot_nt((q * e_in[:, ks]).astype(BF16), st.astype(BF16))
        st_ref[h] = st * e_tot[:, ks] + _dot_tn(v, (k * e_out[:, ks]).astype(BF16))
        o_ref[:, vs] = o


def _gla_kernel(qf, kf, vf, pf, qb, kb, vb, pb, wa_ref, ba_ref, of_ref, ob_ref, st_ref, *, nh):
    @pl.when(pl.program_id(1) == 0)
    def _():
        st_ref[...] = jnp.zeros_like(st_ref)

    for i in range(qf.shape[0]):
        _gla_direction(qf.at[i], kf.at[i], vf.at[i], pf.at[i], wa_ref[0], ba_ref[0], of_ref.at[i], st_ref.at[0, i],
                       nh=nh, backward=False)
        _gla_direction(qb.at[i], kb.at[i], vb.at[i], pb.at[i], wa_ref[1], ba_ref[1], ob_ref.at[i], st_ref.at[1, i],
                       nh=nh, backward=True)


def gla_scan(p, paa, wa_blk, ba_blk, *, nb, rows_per_batch, n_ctx_chunks, nh):
    t = p.shape[0]
    c = GLA_CHUNK
    ncb = rows_per_batch // c
    kw = nh * GLA_DK
    vw = nh * GLA_DV
    gb = GLA_BATCH if nb % GLA_BATCH == 0 else 1

    def fwd(g):
        return g

    def bwd(g):
        return jnp.where(g < n_ctx_chunks, n_ctx_chunks - 1 - g, ncb - 1 + n_ctx_chunks - g)

    def chunk_specs(cm):
        return [
            pl.BlockSpec((gb, c, kw), lambda b, g: (b, cm(g), 0)),
            pl.BlockSpec((gb, c, kw), lambda b, g: (b, cm(g), 1)),
            pl.BlockSpec((gb, c, vw), lambda b, g: (b, cm(g), 2 * kw // vw)),
            pl.BlockSpec((gb, c, LANES), lambda b, g: (b, cm(g), 0)),
        ]

    p3 = p.reshape(nb, rows_per_batch, p.shape[1])
    paa3 = paa.reshape(nb, rows_per_batch, LANES)
    out = jax.ShapeDtypeStruct((nb, rows_per_batch, vw), F32)
    o_f, o_b = pl.pallas_call(
        functools.partial(_gla_kernel, nh=nh),
        out_shape=(out, out),
        grid=(nb // gb, ncb),
        in_specs=chunk_specs(fwd) + chunk_specs(bwd) + [
            pl.BlockSpec((2, LANES, kw), lambda b, g: (0, 0, 0)),
            pl.BlockSpec((2, 1, kw), lambda b, g: (0, 0, 0)),
        ],
        out_specs=(pl.BlockSpec((gb, c, vw), lambda b, g: (b, fwd(g), 0)),
                   pl.BlockSpec((gb, c, vw), lambda b, g: (b, bwd(g), 0))),
        scratch_shapes=[pltpu.VMEM((2, gb, nh, GLA_DV, GLA_DK), F32)],
        compiler_params=_cparams(("arbitrary", "arbitrary")),
        name="gla_scan",
    )(p3, p3, p3, paa3, p3, p3, p3, paa3, wa_blk, ba_blk)
    return o_f.reshape(t, vw), o_b.reshape(t, vw)


def _gla_out_kernel(of_ref, ob_ref, gate_ref, gn_ref, out_ref, *, nh):
    o = of_ref[...] + ob_ref[...]
    for h in range(nh):
        vs = slice(h * GLA_DV, (h + 1) * GLA_DV)
        gate = gate_ref[:, vs].astype(F32)
        out_ref[:, vs] = (_rms(o[:, vs], gn_ref[...]) * _silu(gate)).astype(out_ref.dtype)


def gla_output(o_f, o_b, p, gn, in_tile, *, gate_block, nh):
    vw = nh * GLA_DV
    row = pl.BlockSpec((ROW_TILE, vw), lambda i: (in_tile(i), 0))
    return pl.pallas_call(
        functools.partial(_gla_out_kernel, nh=nh),
        out_shape=jax.ShapeDtypeStruct((in_tile.n * ROW_TILE, vw), BF16),
        grid=(in_tile.n,),
        in_specs=[row, row, pl.BlockSpec((ROW_TILE, vw), lambda i: (in_tile(i), gate_block)),
                  pl.BlockSpec((1, GLA_DV), lambda i: (0, 0))],
        out_specs=pl.BlockSpec((ROW_TILE, vw), lambda i: (i, 0)),
        compiler_params=_cparams(("arbitrary",)),
        name="gla_output",
    )(o_f, o_b, p, gn.reshape(1, GLA_DV))


def _rope(x, cos, sin_signed):
    lane = lax.broadcasted_iota(jnp.int32, x.shape, 1)
    first = (lane % (HEAD_DIM // 2)) < (HEAD_DIM // 4)
    rot = jnp.where(first, pltpu.roll(x, HEAD_DIM - HEAD_DIM // 4, 1), pltpu.roll(x, HEAD_DIM // 4, 1))
    return x * cos + rot * sin_signed


def rope_tables(n_ctx, seq):
    nf = HEAD_DIM // 4
    rows = seq // GRID_W
    row = jnp.repeat(jnp.arange(rows, dtype=jnp.int32), GRID_W).astype(F32)
    col = jnp.tile(jnp.arange(GRID_W, dtype=jnp.int32), rows).astype(F32)
    inv = ROPE_THETA ** (-jnp.arange(nf, dtype=F32) / nf)
    ang = jnp.concatenate([row[:, None] * inv, row[:, None] * inv, col[:, None] * inv, col[:, None] * inv], axis=1)
    sign = jnp.tile(jnp.concatenate([-jnp.ones((nf,), F32), jnp.ones((nf,), F32)]), 2)
    cos = jnp.concatenate([jnp.ones((n_ctx, HEAD_DIM), F32), jnp.cos(ang)], axis=0)
    sin = jnp.concatenate([jnp.zeros((n_ctx, HEAD_DIM), F32), jnp.sin(ang) * sign], axis=0)
    return cos, sin


def _attend(q, keys, vals):
    s = _dot_nt(q, keys)
    e = jnp.exp(s - jnp.max(s, axis=-1, keepdims=True))
    return _dot(e.astype(BF16), vals), jnp.sum(e, axis=-1, keepdims=True)


def _gqa_kernel(q_ref, k_ref, v_ref, cq_ref, sq_ref, ck_ref, sk_ref, qn_ref, kn_ref, o_ref, ks_ref, *,
                n_ctx, ctx_tile_first):
    qi = pl.program_id(2)

    n_kv = k_ref.shape[1] // HEAD_DIM

    @pl.when(qi == 0)
    def _():
        for j in range(n_kv):
            cs = slice(j * HEAD_DIM, (j + 1) * HEAD_DIM)
            k = _rms(k_ref[:, cs].astype(F32), kn_ref[...])
            ks_ref[:, cs] = _rope(k, ck_ref[...], sk_ref[...]).astype(BF16)

    def attend(nk):
        for j in range(n_kv):
            cs = slice(j * HEAD_DIM, (j + 1) * HEAD_DIM)
            vals = v_ref[0:nk, cs]
            for g in range(GQA_GROUP):
                head = j * GQA_GROUP + g
                hs = slice(head * HEAD_DIM, (head + 1) * HEAD_DIM)
                q = _rms(q_ref[:, hs].astype(F32), qn_ref[...])
                q = (_rope(q, cq_ref[...], sq_ref[...]) * (HEAD_DIM ** -0.5)).astype(BF16)
                pv, l = _attend(q, ks_ref[0:nk, cs], vals)
                o_ref[:, hs] = (pv / l).astype(o_ref.dtype)

    if ctx_tile_first:
        @pl.when(qi == 0)
        def _():
            attend(n_ctx)

        @pl.when(qi > 0)
        def _():
            attend(k_ref.shape[0])
    else:
        attend(k_ref.shape[0])


def gqa_attention(p, cos, sin, qn, kn, *, nb, rows_per_batch, n_ctx, n_kv, q_col, k_col, v_col, with_ctx):
    t = p.shape[0]
    tpb = rows_per_batch // ROW_TILE
    nq = tpb if with_ctx else tpb - 1
    off = 0 if with_ctx else 1
    kvw = GQA_KV_PER_STEP * HEAD_DIM
    qw = GQA_GROUP * kvw
    return pl.pallas_call(
        functools.partial(_gqa_kernel, n_ctx=n_ctx, ctx_tile_first=with_ctx),
        out_shape=jax.ShapeDtypeStruct((nb * nq * ROW_TILE, n_kv * GQA_GROUP * HEAD_DIM), BF16),
        grid=(nb, n_kv // GQA_KV_PER_STEP, nq),
        in_specs=[
            pl.BlockSpec((ROW_TILE, qw), lambda b, h, i: (b * tpb + i + off, q_col // qw + h)),
            pl.BlockSpec((rows_per_batch, kvw), lambda b, h, i: (b, k_col // kvw + h)),
            pl.BlockSpec((rows_per_batch, kvw), lambda b, h, i: (b, v_col // kvw + h)),
            pl.BlockSpec((ROW_TILE, HEAD_DIM), lambda b, h, i: (i + off, 0)),
            pl.BlockSpec((ROW_TILE, HEAD_DIM), lambda b, h, i: (i + off, 0)),
            pl.BlockSpec((rows_per_batch, HEAD_DIM), lambda b, h, i: (0, 0)),
            pl.BlockSpec((rows_per_batch, HEAD_DIM), lambda b, h, i: (0, 0)),
            pl.BlockSpec((1, HEAD_DIM), lambda b, h, i: (0, 0)),
            pl.BlockSpec((1, HEAD_DIM), lambda b, h, i: (0, 0)),
        ],
        out_specs=pl.BlockSpec((ROW_TILE, qw), lambda b, h, i: (b * nq + i, h)),
        scratch_shapes=[pltpu.VMEM((rows_per_batch, kvw), BF16)],
        compiler_params=_cparams(("arbitrary", "arbitrary", "arbitrary")),
        name="gqa_attention",
    )(p, p, p, cos, sin, cos, sin, qn.reshape(1, HEAD_DIM), kn.reshape(1, HEAD_DIM))


def _diff_kernel(q_ref, k_ref, v_ref, cq_ref, sq_ref, ck_ref, sk_ref, lq1_ref, lk1_ref, lq2_ref, lk2_ref, gn_ref,
                 o_ref, ks_ref, *, n_ctx, ctx_tile_first, lam_init):
    qi = pl.program_id(1)
    nh = v_ref.shape[1] // DIFF_DV

    @pl.when(qi == 0)
    def _():
        for j in range(2 * nh):
            hs = slice(j * HEAD_DIM, (j + 1) * HEAD_DIM)
            ks_ref[:, hs] = _rope(k_ref[:, hs].astype(F32), ck_ref[...], sk_ref[...]).astype(BF16)

    lam = (jnp.exp(jnp.sum(lq1_ref[...] * lk1_ref[...], axis=-1, keepdims=True))
           - jnp.exp(jnp.sum(lq2_ref[...] * lk2_ref[...], axis=-1, keepdims=True)) + lam_init)

    def attend(nk):
        for h in range(nh):
            vs = slice(h * DIFF_DV, (h + 1) * DIFF_DV)
            vals = v_ref[0:nk, vs]
            o = None
            for j in range(2):
                hs = slice((2 * h + j) * HEAD_DIM, (2 * h + j + 1) * HEAD_DIM)
                q = (_rope(q_ref[:, hs].astype(F32), cq_ref[...], sq_ref[...]) * (HEAD_DIM ** -0.5)).astype(BF16)
                pv, l = _attend(q, ks_ref[0:nk, hs], vals)
                o = pv / l if j == 0 else o - (lam / l) * pv
            o_ref[:, vs] = (_rms(o, gn_ref[...]) * (1.0 - lam_init)).astype(o_ref.dtype)

    if ctx_tile_first:
        @pl.when(qi == 0)
        def _():
            attend(n_ctx)

        @pl.when(qi > 0)
        def _():
            attend(k_ref.shape[0])
    else:
        attend(k_ref.shape[0])


def diff_attention(p, cos, sin, lq1, lk1, lq2, lk2, gn, *, nb, rows_per_batch, n_ctx, nh, q_col, k_col, v_col,
                   with_ctx, lam_init):
    t = p.shape[0]
    tpb = rows_per_batch // ROW_TILE
    nq = tpb if with_ctx else tpb - 1
    off = 0 if with_ctx else 1
    w = nh * DIFF_DV
    vec = pl.BlockSpec((1, HEAD_DIM), lambda b, i: (0, 0))
    return pl.pallas_call(
        functools.partial(_diff_kernel, n_ctx=n_ctx, ctx_tile_first=with_ctx, lam_init=lam_init),
        out_shape=jax.ShapeDtypeStruct((nb * nq * ROW_TILE, w), BF16),
        grid=(nb, nq),
        in_specs=[
            pl.BlockSpec((ROW_TILE, w), lambda b, i: (b * tpb + i + off, q_col // w)),
            pl.BlockSpec((rows_per_batch, w), lambda b, i: (b, k_col // w)),
            pl.BlockSpec((rows_per_batch, w), lambda b, i: (b, v_col // w)),
            pl.BlockSpec((ROW_TILE, HEAD_DIM), lambda b, i: (i + off, 0)),
            pl.BlockSpec((ROW_TILE, HEAD_DIM), lambda b, i: (i + off, 0)),
            pl.BlockSpec((rows_per_batch, HEAD_DIM), lambda b, i: (0, 0)),
            pl.BlockSpec((rows_per_batch, HEAD_DIM), lambda b, i: (0, 0)),
            vec, vec, vec, vec,
            pl.BlockSpec((1, DIFF_DV), lambda b, i: (0, 0)),
        ],
        out_specs=pl.BlockSpec((ROW_TILE, w), lambda b, i: (b * nq + i, 0)),
        scratch_shapes=[pltpu.VMEM((rows_per_batch, w), BF16)],
        compiler_params=_cparams(("arbitrary", "arbitrary")),
        name="diff_attention",
    )(p, p, p, cos, sin, cos, sin, lq1.reshape(1, -1), lk1.reshape(1, -1), lq2.reshape(1, -1), lk2.reshape(1, -1),
      gn.reshape(1, DIFF_DV))


def _route_kernel(lg_ref, bias_ref, o_ref):
    x = lg_ref[...] + bias_ref[...]
    lane = lax.broadcasted_iota(jnp.int32, x.shape, 1).astype(F32)
    neg = -jnp.inf

    def first_max(vals, mask):
        v = jnp.where(mask, vals, neg)
        m = jnp.max(v, axis=-1, keepdims=True)
        idx = jnp.min(jnp.where(mask & (v == m), lane, float(LANES)), axis=-1, keepdims=True)
        return m, idx

    gmask = lane < N_GROUPS
    gm, gidx = first_max(x, gmask)
    g_w = 1.0 / jnp.sum(jnp.where(gmask, jnp.exp(x - gm), 0.0), axis=-1, keepdims=True)
    lo = N_GROUPS + EXPERTS_PER_GROUP * gidx
    emask = (lane >= lo) & (lane < lo + EXPERTS_PER_GROUP)
    m1, i1 = first_max(x, emask)
    m2, i2 = first_max(x, emask & (lane != i1))
    r = jnp.exp(m2 - m1)
    w1 = g_w / (1.0 + r)
    w2 = g_w * r / (1.0 + r)
    out = jnp.where(lane == 0, i1 - N_GROUPS, 0.0)
    out = jnp.where(lane == 1, i2 - N_GROUPS, out)
    out = jnp.where(lane == 2, w1, out)
    out = jnp.where(lane == 3, w2, out)
    o_ref[...] = out


def route(logits, bias):
    t = logits.shape[0]
    return pl.pallas_call(
        _route_kernel,
        out_shape=jax.ShapeDtypeStruct((t, LANES), F32),
        grid=(t // ROW_TILE,),
        in_specs=[pl.BlockSpec((ROW_TILE, LANES), lambda i: (i, 0)), pl.BlockSpec((1, LANES), lambda i: (0, 0))],
        out_specs=pl.BlockSpec((ROW_TILE, LANES), lambda i: (i, 0)),
        compiler_params=_cparams(("arbitrary",)),
        name="route",
    )(logits, bias)


def dispatch_plan(ids):
    t = ids.shape[0]
    tps = SEG_ROWS // SEG_TILE
    n_src = 2 * t + N_EXPERTS * SEG_TILE
    n_seg = (n_src // SEG_TILE + (tps - 1) * N_EXPERTS) // tps + 1 + (N_EXPERTS * SEG_TILE) // SEG_ROWS + 2
    e = ids.reshape(-1)
    onehot = (e[:, None] == jnp.arange(N_EXPERTS, dtype=jnp.int32)[None, :]).astype(jnp.int32)
    csum = jnp.cumsum(onehot, axis=0)
    counts = csum[-1]
    tiles = (counts + SEG_TILE - 1) // SEG_TILE
    segs = (tiles + tps - 1) // tps
    seg_end = jnp.cumsum(segs)
    seg_start = seg_end - segs
    row_start = (jnp.cumsum(tiles) - tiles) * SEG_TILE
    rank = csum - 1
    cpos = jnp.sum(onehot * (rank + row_start[None, :]), axis=1)
    src = jnp.zeros((n_src,), jnp.int32).at[cpos].set(jnp.arange(2 * t, dtype=jnp.int32) // 2)
    sidx = jnp.arange(n_seg, dtype=jnp.int32)
    seg_expert = jnp.minimum(jnp.sum((seg_end[None, :] <= sidx[:, None]).astype(jnp.int32), axis=1), N_EXPERTS - 1)
    sel = (seg_expert[:, None] == jnp.arange(N_EXPERTS, dtype=jnp.int32)[None, :]).astype(jnp.int32)
    k = sidx - jnp.sum(sel * seg_start[None, :], axis=1)
    seg_row0 = jnp.sum(sel * row_start[None, :], axis=1) + k * SEG_ROWS
    used = sidx < seg_end[-1]
    seg_nt = jnp.where(used, jnp.clip(jnp.sum(sel * tiles[None, :], axis=1) - k * tps, 0, tps), 0)
    y_rows = n_src + SEG_ROWS
    fill0 = jnp.max(jnp.where(used, seg_row0 + SEG_ROWS, 0)) + (sidx - seg_end[-1]) * SEG_ROWS
    seg_row0 = jnp.where(used, seg_row0, jnp.minimum(fill0, y_rows - SEG_ROWS))
    seg_fill = jnp.where(used, 0, (fill0 < y_rows).astype(jnp.int32))
    return src, cpos.reshape(t, 2), seg_expert, seg_row0, seg_nt, seg_fill, seg_end[-1:]


def _row_copy(src_hbm, idx, dst_vmem, r, sem):
    return pltpu.make_async_copy(src_hbm.at[pl.ds(idx, 1)], dst_vmem.at[pl.ds(r, 1)], sem)


def _expert_kernel(se_ref, r0_ref, nt_ref, fill_ref, nseg_ref, src_ref, h_hbm, wg_hbm, wu_hbm, wd_hbm, y_hbm,
                   land, xs, acc_a, acc_u, hid, st_g, st_u, st_d, wb_g, wb_u, wb_d, yst,
                   sem_x, sem_g, sem_u, sem_d, sem_y, *, expert_base):
    s = pl.program_id(0)
    n_seg = nseg_ref[0]
    d = xs.shape[1]
    ck = st_g.shape[1]
    cn = st_d.shape[2]
    n_k = d // ck
    n_c = d // cn
    assert n_k > W_SLOTS and n_c > W_SLOTS and n_c >= 3

    def start_gather(seg):
        r0 = r0_ref[seg]

        def tile(tt, carry):
            base = tt * SEG_TILE

            def body(r, c):
                _row_copy(h_hbm, src_ref[r0 + base + r], land, base + r, sem_x).start()
                return c

            return lax.fori_loop(0, SEG_TILE, body, carry, unroll=8)

        lax.fori_loop(0, nt_ref[seg], tile, 0)

    def finish_gather(seg):
        nt = nt_ref[seg]

        def wait(t, carry):
            pltpu.make_async_copy(h_hbm.at[pl.ds(0, SEG_TILE)], land.at[pl.ds(0, SEG_TILE)], sem_x).wait()
            return carry

        lax.fori_loop(0, nt, wait, 0)

        def convert(t, carry):
            rows = pl.ds(pl.multiple_of(t * SEG_TILE, SEG_TILE), SEG_TILE)
            xs[rows, :] = land[rows, :].astype(BF16)
            return carry

        lax.fori_loop(0, nt, convert, 0)

    def in_copies(e, kc):
        slot = kc % W_SLOTS
        rows = pl.ds(pl.multiple_of(kc * ck, ck), ck)
        return (pltpu.make_async_copy(wg_hbm.at[e, rows], st_g.at[slot], sem_g.at[slot]),
                pltpu.make_async_copy(wu_hbm.at[e, rows], st_u.at[slot], sem_u.at[slot]))

    def out_copy(e, c):
        slot = c % W_SLOTS
        cols = pl.ds(pl.multiple_of(c * cn, cn), cn)
        return pltpu.make_async_copy(wd_hbm.at[e, :, cols], st_d.at[slot], sem_d.at[slot])

    def y_copy(c, slot=None):
        slot = c % 2 if slot is None else slot
        cols = pl.ds(pl.multiple_of(c * cn, cn), cn)
        rows = pl.ds(pl.multiple_of(r0_ref[s], SEG_TILE), SEG_ROWS)
        return pltpu.make_async_copy(yst.at[slot], y_hbm.at[rows, cols], sem_y.at[slot])

    def land_in(e, kc):
        for cp in in_copies(e, kc):
            cp.wait()
        wb_g[kc % 2] = st_g[kc % W_SLOTS].astype(BF16)
        wb_u[kc % 2] = st_u[kc % W_SLOTS].astype(BF16)

        @pl.when(kc + W_SLOTS < n_k)
        def _():
            for cp in in_copies(e, kc + W_SLOTS):
                cp.start()

    def land_out(e, c):
        out_copy(e, c).wait()
        wb_d[c % 2] = st_d[c % W_SLOTS].astype(BF16)

        @pl.when(c + W_SLOTS < n_c)
        def _():
            out_copy(e, c + W_SLOTS).start()

    def gate_up(kc):
        x_k = xs[:, pl.ds(pl.multiple_of(kc * ck, ck), ck)]
        acc_a[...] += _dot(x_k, wb_g[kc % 2])
        acc_u[...] += _dot(x_k, wb_u[kc % 2])

    def down(c):
        yst[c % 2] = _dot(hid[...], wb_d[c % 2])
        y_copy(c).start()

    @pl.when(s == 0)
    def _():
        xs[...] = jnp.zeros_like(xs)
        start_gather(0)

    def start_first_in(seg):
        for i in range(W_SLOTS):
            for cp in in_copies(expert_base + se_ref[seg], i):
                cp.start()

    def start_first_out(seg):
        for i in range(W_SLOTS):
            out_copy(expert_base + se_ref[seg], i).start()

    @pl.when(s == 0)
    def _():
        start_first_in(0)
        start_first_out(0)

    @pl.when(s < n_seg)
    def _():
        e = expert_base + se_ref[s]
        finish_gather(s)

        @pl.when(s + 1 < n_seg)
        def _():
            start_gather(s + 1)

        acc_a[...] = jnp.zeros_like(acc_a)
        acc_u[...] = jnp.zeros_like(acc_u)
        land_in(e, 0)

        def in_step(kc, carry):
            land_in(e, kc + 1)
            gate_up(kc)
            return carry

        lax.fori_loop(0, n_k - 1, in_step, 0)

        @pl.when(s + 1 < n_seg)
        def _():
            start_first_in(s + 1)

        land_out(e, 0)
        gate_up(n_k - 1)
        hid[...] = (_silu(acc_a[...]) * acc_u[...]).astype(BF16)

        def out_step(c, carry):
            @pl.when(c >= 2)
            def _():
                y_copy(c - 2).wait()

            land_out(e, c + 1)
            down(c)
            return carry

        lax.fori_loop(0, n_c - 1, out_step, 0)

        @pl.when(s + 1 < n_seg)
        def _():
            start_first_out(s + 1)

        y_copy(n_c - 3).wait()
        down(n_c - 1)
        y_copy(n_c - 2).wait()
        y_copy(n_c - 1).wait()

    @pl.when(fill_ref[s] == 1)
    def _():
        yst[0] = jnp.zeros_like(yst[0])
        for c in range(n_c):
            y_copy(c, 0).start()
        for c in range(n_c):
            y_copy(c, 0).wait()


def expert_mlp(h, plan, wg, wu, wd, *, expert_base):
    src, _, seg_expert, seg_row0, seg_nt, seg_fill, n_seg = plan
    d = h.shape[1]
    f = wg.shape[2]
    n_grid = seg_expert.shape[0]
    ck = d // W_CHUNKS
    cn = d // W_CHUNKS
    return pl.pallas_call(
        functools.partial(_expert_kernel, expert_base=expert_base),
        out_shape=jax.ShapeDtypeStruct((src.shape[0] + SEG_ROWS, d), F32),
        grid_spec=pltpu.PrefetchScalarGridSpec(
            num_scalar_prefetch=6,
            grid=(n_grid,),
            in_specs=[pl.BlockSpec(memory_space=pl.ANY)] * 4,
            out_specs=pl.BlockSpec(memory_space=pl.ANY),
            scratch_shapes=[
                pltpu.VMEM((SEG_ROWS, d), F32), pltpu.VMEM((SEG_ROWS, d), BF16),
                pltpu.VMEM((SEG_ROWS, f), F32), pltpu.VMEM((SEG_ROWS, f), F32), pltpu.VMEM((SEG_ROWS, f), BF16),
                pltpu.VMEM((W_SLOTS, ck, f), F32), pltpu.VMEM((W_SLOTS, ck, f), F32), pltpu.VMEM((W_SLOTS, f, cn), F32),
                pltpu.VMEM((2, ck, f), BF16), pltpu.VMEM((2, ck, f), BF16), pltpu.VMEM((2, f, cn), BF16),
                pltpu.VMEM((2, SEG_ROWS, cn), F32),
                pltpu.SemaphoreType.DMA, pltpu.SemaphoreType.DMA((W_SLOTS,)), pltpu.SemaphoreType.DMA((W_SLOTS,)),
                pltpu.SemaphoreType.DMA((W_SLOTS,)), pltpu.SemaphoreType.DMA((2,)),
            ],
        ),
        compiler_params=_cparams(("arbitrary",)),
        name="moe_experts",
    )(seg_expert, seg_row0, seg_nt, seg_fill, n_seg, src, h, wg, wu, wd)


def _combine_kernel(p1_ref, p2_ref, x_ref, g_ref, r_ref, fg_ref, y_hbm, o_ref, buf_ref, sem, *, final_norm):
    i = pl.program_id(0)
    rows = x_ref.shape[0]
    slot = i % 2

    def start_tile(tile, sl):
        def start(r, carry):
            _row_copy(y_hbm, p1_ref[tile * rows + r], buf_ref.at[sl, 0], r, sem.at[sl]).start()
            _row_copy(y_hbm, p2_ref[tile * rows + r], buf_ref.at[sl, 1], r, sem.at[sl]).start()
            return carry

        lax.fori_loop(0, rows, start, 0, unroll=8)

    @pl.when(i == 0)
    def _():
        start_tile(0, 0)

    @pl.when(i + 1 < pl.num_programs(0))
    def _():
        start_tile(i + 1, 1 - slot)

    pltpu.make_async_copy(y_hbm.at[pl.ds(0, rows)], buf_ref.at[slot, 0], sem.at[slot]).wait()
    pltpu.make_async_copy(y_hbm.at[pl.ds(0, rows)], buf_ref.at[slot, 1], sem.at[slot]).wait()
    w1 = r_ref[:, 2:3]
    w2 = r_ref[:, 3:4]
    out = x_ref[...] + g_ref[0] * (w1 * buf_ref[slot, 0] + w2 * buf_ref[slot, 1])
    o_ref[...] = _rms(out, fg_ref[...]) if final_norm else out


def moe_combine(x, y, pos, routed, mod3, gate_chunk, mod_row, final_g=None):
    t, d = x.shape
    fg = jnp.ones((1, d), F32) if final_g is None else final_g.reshape(1, d)
    return pl.pallas_call(
        functools.partial(_combine_kernel, final_norm=final_g is not None),
        out_shape=jax.ShapeDtypeStruct((t, d), F32),
        grid_spec=pltpu.PrefetchScalarGridSpec(
            num_scalar_prefetch=2,
            grid=(t // ROW_TILE,),
            in_specs=[
                pl.BlockSpec((ROW_TILE, d), lambda i, a, b: (i, 0)),
                pl.BlockSpec((1, 1, d), lambda i, a, b: (mod_row(i), 0, gate_chunk)),
                pl.BlockSpec((ROW_TILE, LANES), lambda i, a, b: (i, 0)),
                pl.BlockSpec((1, d), lambda i, a, b: (0, 0)),
                pl.BlockSpec(memory_space=pl.ANY),
            ],
            out_specs=pl.BlockSpec((ROW_TILE, d), lambda i, a, b: (i, 0)),
            scratch_shapes=[pltpu.VMEM((2, 2, ROW_TILE, d), F32), pltpu.SemaphoreType.DMA((2,))],
        ),
        compiler_params=_cparams(("arbitrary",)),
        name="moe_combine",
    )(pos[:, 0], pos[:, 1], x, mod3, routed, fg, y)


def kernel(x, c, ctx, c_ctx, ada_w, ada_b, norm1_g, w_in, gla_wa_fwd, gla_ba_fwd, gla_wa_bwd, gla_ba_bwd, gla_norm_g,
           gqa_qnorm_g, gqa_knorm_g, diff_lq1, diff_lk1, diff_lq2, diff_lk2, diff_norm_g, w_out, norm2_g, router_wg,
           router_bg, router_we, router_be, moe_w_gate, moe_w_up, moe_w_down, final_norm_g):
    nb, seq, d = x.shape
    n_ctx = ctx.shape[1]
    depth = ada_w.shape[0]
    assert n_ctx == ROW_TILE and seq % ROW_TILE == 0 and seq % GRID_W == 0
    rpb = n_ctx + seq
    tpb = rpb // ROW_TILE
    t = nb * rpb

    gla_heads = (d // 4) // GLA_DV
    gqa_heads = (d // 2) // HEAD_DIM
    n_kv = gqa_heads // GQA_GROUP
    diff_heads = (d // 4) // DIFF_DV
    kw = gla_heads * GLA_DK
    vw = gla_heads * GLA_DV
    splits = (kw, kw, vw, vw, GLA_RANK, GLA_RANK, gqa_heads * HEAD_DIM, n_kv * HEAD_DIM, n_kv * HEAD_DIM,
              diff_heads * 2 * HEAD_DIM, diff_heads * 2 * HEAD_DIM, diff_heads * DIFF_DV)
    offs = [0]
    for s in splits:
        offs.append(offs[-1] + s)
    col = {}
    acc = 0
    for name, width in (("aq", kw), ("ak", kw), ("av", vw), ("ag", vw), ("bq", splits[6]), ("bk", splits[7]),
                        ("bv", splits[8]), ("dq", splits[9]), ("dk", splits[10]), ("dv", splits[11])):
        col[name] = acc
        acc += width

    def mod_row_full(i):
        return jnp.where(i % tpb == 0, nb, i // tpb)

    def mod_row_latent(i):
        return i // (tpb - 1)

    tokens = jnp.concatenate([ctx, x], axis=1).reshape(t, d)
    c_all = jnp.zeros((8, d), F32).at[:nb].set(c).at[nb].set(c_ctx)
    mod = ada_modulation(c_all, ada_w, ada_b)
    cos, sin = rope_tables(n_ctx, seq)
    w_wide, w_dec_all = split_rows_transpose(jnp.swapaxes(w_in, 1, 2), offs[4], offs[6])
    w_out_bf = w_out.astype(BF16)
    f = moe_w_gate.shape[-1]
    wg_all = moe_w_gate.reshape(depth * N_EXPERTS, d, f)
    wu_all = moe_w_up.reshape(depth * N_EXPERTS, d, f)
    wd_all = moe_w_down.reshape(depth * N_EXPERTS, f, d)

    for l in range(depth):
        last = l == depth - 1
        lam_init = 0.8 - 0.6 * math.exp(-0.3 * l)
        mod3 = mod[l].reshape(8, 1, 6 * d)

        h, paa = norm_modulate(tokens, norm1_g[l], mod3, 1, 0, w_dec_all[l], mod_row_full, h_dtype=BF16,
                               exact_small=False)
        p = matmul(h, w_wide, l, bm=1024, bn=1024, out_dtype=BF16)

        wa_blk = jnp.zeros((2, LANES, kw), F32)
        wa_blk = wa_blk.at[0, :GLA_RANK].set(gla_wa_fwd[l]).at[1, GLA_RANK:2 * GLA_RANK].set(gla_wa_bwd[l])
        ba_blk = jnp.stack([gla_ba_fwd[l], gla_ba_bwd[l]]).reshape(2, 1, kw)
        o_f, o_b = gla_scan(p, paa, wa_blk, ba_blk, nb=nb, rows_per_batch=rpb, n_ctx_chunks=n_ctx // GLA_CHUNK,
                            nh=gla_heads)
        if last:
            x_tile, mod_row = _LatentTiles(nb, tpb), mod_row_latent
        else:
            x_tile, mod_row = _AllTiles(t // ROW_TILE), mod_row_full
        mix_a = gla_output(o_f, o_b, p, gla_norm_g[l], x_tile, gate_block=col["ag"] // vw, nh=gla_heads)
        mix_b = gqa_attention(p, cos, sin, gqa_qnorm_g[l], gqa_knorm_g[l], nb=nb, rows_per_batch=rpb, n_ctx=n_ctx,
                              n_kv=n_kv, q_col=col["bq"], k_col=col["bk"], v_col=col["bv"], with_ctx=not last)
        mix_c = diff_attention(p, cos, sin, diff_lq1[l], diff_lk1[l], diff_lq2[l], diff_lk2[l], diff_norm_g[l],
                               nb=nb, rows_per_batch=rpb, n_ctx=n_ctx, nh=diff_heads, q_col=col["dq"],
                               k_col=col["dk"], v_col=col["dv"], with_ctx=not last, lam_init=lam_init)
        tokens = matmul3_gated_residual(mix_a, mix_b, mix_c, w_out_bf, l, tokens, mod3, 2, x_tile, mod_row, bn=1024)

        w_route = jnp.zeros((d, LANES), F32).at[:, :N_GROUPS].set(router_wg[l])
        w_route = w_route.at[:, N_GROUPS:N_GROUPS + N_EXPERTS].set(router_we[l])
        b_route = jnp.zeros((1, LANES), F32).at[0, :N_GROUPS].set(router_bg[l])
        b_route = b_route.at[0, N_GROUPS:N_GROUPS + N_EXPERTS].set(router_be[l])
        h2, logits = norm_modulate(tokens, norm2_g[l], mod3, 4, 3, w_route, mod_row, h_dtype=F32, exact_small=True)
        routed = route(logits, b_route)
        plan = dispatch_plan(routed[:, 0:2].astype(jnp.int32))
        y = expert_mlp(h2, plan, wg_all, wu_all, wd_all, expert_base=l * N_EXPERTS)
        tokens = moe_combine(tokens, y, plan[1], routed, mod3, 5, mod_row, final_norm_g if last else None)

    return tokens.reshape(nb, seq, d)
```

```python
import functools
import math

import jax
import jax.numpy as jnp
from jax import lax
from jax.experimental import pallas as pl
from jax.experimental.pallas import tpu as pltpu

F32 = jnp.float32
BF16 = jnp.bfloat16

HEAD_DIM = 128
GRID_W = 64
ROPE_THETA = 10000.0
EPS = 1e-6
GLA_DK = HEAD_DIM
GLA_DV = 2 * HEAD_DIM
GLA_RANK = 16
GLA_TAU = 16.0
GLA_CHUNK = 64
GQA_GROUP = 4
GQA_KV_PER_STEP = 2
DIFF_DV = 2 * HEAD_DIM
N_GROUPS = 4
EXPERTS_PER_GROUP = 8
N_EXPERTS = N_GROUPS * EXPERTS_PER_GROUP

LANES = 128
ROW_TILE = 256
SEG_TILE = 128
SEG_ROWS = 768
W_CHUNKS = 8
W_SLOTS = 3
VMEM_LIMIT = 56 * 1024 * 1024


def _cparams(sem, vmem=VMEM_LIMIT):
    return pltpu.CompilerParams(dimension_semantics=sem, vmem_limit_bytes=vmem)


def _sigmoid(x):
    return 1.0 / (1.0 + jnp.exp(-x))


def _silu(x):
    return x * _sigmoid(x)


def _log_sigmoid(x):
    return jnp.minimum(x, 0.0) - jnp.log(1.0 + jnp.exp(-jnp.abs(x)))


def _rms(x, g):
    return x * lax.rsqrt(jnp.mean(x * x, axis=-1, keepdims=True) + EPS) * g


def _dot(a, b):
    return jnp.dot(a, b, preferred_element_type=F32)


def _dot_nt(a, b):
    return lax.dot_general(a, b, (((1,), (1,)), ((), ())), preferred_element_type=F32)


def _dot_tn(a, b):
    return lax.dot_general(a, b, (((0,), (0,)), ((), ())), preferred_element_type=F32)


class _AllTiles:
    def __init__(self, n):
        self.n = n

    def __call__(self, i):
        return i


class _LatentTiles:
    def __init__(self, nb, tiles_per_batch):
        self.n = nb * (tiles_per_batch - 1)
        self._lat = tiles_per_batch - 1
        self._tpb = tiles_per_batch

    def __call__(self, i):
        return (i // self._lat) * self._tpb + 1 + i % self._lat


def _ada_kernel(c_ref, w_ref, b_ref, o_ref):
    s = _silu(c_ref[...]).astype(BF16)
    o_ref[0] = _dot(s, w_ref[0].astype(BF16)) + b_ref[0]


def ada_modulation(c_all, ada_w, ada_b, *, tn=512):
    nl, d, n = ada_w.shape
    return pl.pallas_call(
        _ada_kernel,
        out_shape=jax.ShapeDtypeStruct((nl, 8, n), F32),
        grid=(nl, n // tn),
        in_specs=[
            pl.BlockSpec((8, d), lambda l, j: (0, 0)),
            pl.BlockSpec((1, d, tn), lambda l, j: (l, 0, j)),
            pl.BlockSpec((1, 1, tn), lambda l, j: (l, 0, j)),
        ],
        out_specs=pl.BlockSpec((1, 8, tn), lambda l, j: (l, 0, j)),
        compiler_params=_cparams(("arbitrary", "arbitrary")),
        name="ada_modulation",
    )(c_all, ada_w, ada_b.reshape(nl, 1, n))


def _norm_kernel(x_ref, g_ref, sc_ref, sh_ref, w_ref, h_ref, s_ref, *, exact_small):
    h = _rms(x_ref[...], g_ref[...]) * (1.0 + sc_ref[0]) + sh_ref[0]
    h_ref[...] = h.astype(h_ref.dtype)
    if exact_small:
        w = w_ref[...]
        h_hi, w_hi = h.astype(BF16), w.astype(BF16)
        h_lo, w_lo = (h - h_hi.astype(F32)).astype(BF16), (w - w_hi.astype(F32)).astype(BF16)
        s_ref[...] = _dot(h_hi, w_hi) + _dot(h_hi, w_lo) + _dot(h_lo, w_hi)
    else:
        s_ref[...] = _dot(h.astype(BF16), w_ref[...].astype(BF16))


def norm_modulate(x, g, mod3, sc_chunk, sh_chunk, w_small, mod_row, *, h_dtype, exact_small):
    t, d = x.shape
    ns = w_small.shape[1]
    return pl.pallas_call(
        functools.partial(_norm_kernel, exact_small=exact_small),
        out_shape=(jax.ShapeDtypeStruct((t, d), h_dtype), jax.ShapeDtypeStruct((t, ns), F32)),
        grid=(t // ROW_TILE,),
        in_specs=[
            pl.BlockSpec((ROW_TILE, d), lambda i: (i, 0)),
            pl.BlockSpec((1, d), lambda i: (0, 0)),
            pl.BlockSpec((1, 1, d), lambda i: (mod_row(i), 0, sc_chunk)),
            pl.BlockSpec((1, 1, d), lambda i: (mod_row(i), 0, sh_chunk)),
            pl.BlockSpec((d, ns), lambda i: (0, 0)),
        ],
        out_specs=(pl.BlockSpec((ROW_TILE, d), lambda i: (i, 0)), pl.BlockSpec((ROW_TILE, ns), lambda i: (i, 0))),
        compiler_params=_cparams(("arbitrary",)),
        name="norm_modulate",
    )(x, g.reshape(1, d), mod3, mod3, w_small)


def _split_transpose_kernel(w_ref, cut_ref, o_ref, oc_ref, *, n_cut):
    o_ref[0] = w_ref[0].T.astype(o_ref.dtype)

    @pl.when(pl.program_id(1) == 0)
    def _():
        cut = cut_ref[0].T
        lane = lax.broadcasted_iota(jnp.int32, cut.shape, 1)
        oc_ref[0] = jnp.where(lane < n_cut, cut, 0.0)


def split_rows_transpose(w_t, lo, hi, *, bn=512):
    nl, n, k = w_t.shape
    n_cut = hi - lo
    n_out = n - n_cut
    assert lo % bn == 0 and n_out % bn == 0 and n_cut % 8 == 0 and n_cut <= LANES and lo + LANES <= n

    def src_row(j):
        return pl.multiple_of(j * bn + jnp.where(j * bn >= lo, n_cut, 0), 8)

    def element_rows(rows, row_map):
        return pl.BlockSpec((pl.Element(1), pl.Element(rows), pl.Element(k)), lambda l, j: (l, row_map(j), 0))

    return pl.pallas_call(
        functools.partial(_split_transpose_kernel, n_cut=n_cut),
        out_shape=(jax.ShapeDtypeStruct((nl, k, n_out), BF16), jax.ShapeDtypeStruct((nl, k, LANES), F32)),
        grid=(nl, n_out // bn),
        in_specs=[element_rows(bn, src_row), element_rows(LANES, lambda j: lo)],
        out_specs=(pl.BlockSpec((1, k, bn), lambda l, j: (l, 0, j)), pl.BlockSpec((1, k, LANES), lambda l, j: (l, 0, 0))),
        compiler_params=_cparams(("arbitrary", "arbitrary")),
        name="split_rows_transpose",
    )(w_t, w_t)


def _mm_kernel(a_ref, b_ref, o_ref):
    o_ref[...] = _dot(a_ref[...], b_ref[...]).astype(o_ref.dtype)


def matmul(a, b_all, layer, *, bm, bn, out_dtype):
    m, k = a.shape
    n = b_all.shape[2]
    return pl.pallas_call(
        _mm_kernel,
        out_shape=jax.ShapeDtypeStruct((m, n), out_dtype),
        grid=(m // bm, n // bn),
        in_specs=[pl.BlockSpec((bm, k), lambda i, j: (i, 0)), pl.BlockSpec((None, k, bn), lambda i, j: (layer, 0, j))],
        out_specs=pl.BlockSpec((bm, bn), lambda i, j: (i, j)),
        compiler_params=_cparams(("arbitrary", "arbitrary")),
        name="matmul",
    )(a, b_all)


def _mm3_res_kernel(a1_ref, a2_ref, a3_ref, b_ref, x_ref, g_ref, o_ref):
    k1 = a1_ref.shape[1]
    k2 = k1 + a2_ref.shape[1]
    acc = _dot(a1_ref[...], b_ref[0:k1, :]) + _dot(a2_ref[...], b_ref[k1:k2, :]) + _dot(a3_ref[...], b_ref[k2:, :])
    o_ref[...] = x_ref[...] + g_ref[0] * acc


def matmul3_gated_residual(a1, a2, a3, b_all, layer, x, mod3, gate_chunk, x_tile, mod_row, *, bn):
    k = b_all.shape[1]
    n = b_all.shape[2]
    nbn = n // bn
    return pl.pallas_call(
        _mm3_res_kernel,
        out_shape=jax.ShapeDtypeStruct((x_tile.n * ROW_TILE, n), F32),
        grid=(nbn, x_tile.n),
        in_specs=[
            pl.BlockSpec((ROW_TILE, a1.shape[1]), lambda j, i: (i, 0)),
            pl.BlockSpec((ROW_TILE, a2.shape[1]), lambda j, i: (i, 0)),
            pl.BlockSpec((ROW_TILE, a3.shape[1]), lambda j, i: (i, 0)),
            pl.BlockSpec((None, k, bn), lambda j, i: (layer, 0, j)),
            pl.BlockSpec((ROW_TILE, bn), lambda j, i: (x_tile(i), j)),
            pl.BlockSpec((1, 1, bn), lambda j, i: (mod_row(i), 0, gate_chunk * nbn + j)),
        ],
        out_specs=pl.BlockSpec((ROW_TILE, bn), lambda j, i: (i, j)),
        compiler_params=_cparams(("arbitrary", "arbitrary")),
        name="matmul3_gated_residual",
    )(a1, a2, a3, b_all, x, mod3)


def _gla_direction(q_ref, k_ref, v_ref, paa_ref, wa, ba, o_ref, st_ref, *, nh, backward):
    c = GLA_CHUNK
    row = lax.broadcasted_iota(jnp.int32, (c, c), 0)
    col = lax.broadcasted_iota(jnp.int32, (c, c), 1)
    incl = (col >= row) if backward else (col <= row)
    tri = incl.astype(BF16)

    z = _dot(paa_ref[...].astype(BF16), wa.astype(BF16)) + ba
    la = _log_sigmoid(z) * (1.0 / GLA_TAU)
    la_hi = la.astype(BF16)
    la_lo = (la - la_hi.astype(F32)).astype(BF16)
    b = _dot(tri, la_hi) + _dot(tri, la_lo)
    tot = jnp.sum(la, axis=0, keepdims=True)
    mid = 0.5 * tot
    e_q = jnp.exp(b - mid)
    e_k = jnp.exp(mid - b)
    e_in = jnp.exp(b)
    e_out = jnp.exp(tot - b)
    e_tot = jnp.exp(tot)

    for h in range(nh):
        ks = slice(h * GLA_DK, (h + 1) * GLA_DK)
        vs = slice(h * GLA_DV, (h + 1) * GLA_DV)
        q = q_ref[:, ks].astype(F32) * (GLA_DK ** -0.5)
        k = k_ref[:, ks].astype(F32)
        v = v_ref[:, vs]
        s = _dot_nt((q * e_q[:, ks]).astype(BF16), (k * e_k[:, ks]).astype(BF16))
        s = jnp.where(incl, s, 0.0)
        st = st_ref[h]
        o = _dot(s.astype(BF16), v) + _dot_nt((q * e_in[:, ks]).astype(BF16), st.astype(BF16))
        st_ref[h] = st * e_tot[:, ks] + _dot_tn(v, (k * e_out[:, ks]).astype(BF16))
        o_ref[:, vs] = o


def _gla_kernel(qf, kf, vf, pf, qb, kb, vb, pb, wa_ref, ba_ref, of_ref, ob_ref, st_ref, *, nh):
    @pl.when(pl.program_id(1) == 0)
    def _():
        st_ref[...] = jnp.zeros_like(st_ref)

    _gla_direction(qf, kf, vf, pf, wa_ref[0], ba_ref[0], of_ref, st_ref.at[0], nh=nh, backward=False)
    _gla_direction(qb, kb, vb, pb, wa_ref[1], ba_ref[1], ob_ref, st_ref.at[1], nh=nh, backward=True)


def gla_scan(p, paa, wa_blk, ba_blk, *, nb, rows_per_batch, n_ctx_chunks, nh):
    t = p.shape[0]
    c = GLA_CHUNK
    ncb = rows_per_batch // c
    kw = nh * GLA_DK
    vw = nh * GLA_DV

    def fwd(b, g):
        return b * ncb + g

    def bwd(b, g):
        return b * ncb + jnp.where(g < n_ctx_chunks, n_ctx_chunks - 1 - g, ncb - 1 + n_ctx_chunks - g)

    def chunk_specs(cm):
        return [
            pl.BlockSpec((c, kw), lambda b, g: (cm(b, g), 0)),
            pl.BlockSpec((c, kw), lambda b, g: (cm(b, g), 1)),
            pl.BlockSpec((c, vw), lambda b, g: (cm(b, g), 2 * kw // vw)),
            pl.BlockSpec((c, LANES), lambda b, g: (cm(b, g), 0)),
        ]

    out = jax.ShapeDtypeStruct((t, vw), F32)
    return pl.pallas_call(
        functools.partial(_gla_kernel, nh=nh),
        out_shape=(out, out),
        grid=(nb, ncb),
        in_specs=chunk_specs(fwd) + chunk_specs(bwd) + [
            pl.BlockSpec((2, LANES, kw), lambda b, g: (0, 0, 0)),
            pl.BlockSpec((2, 1, kw), lambda b, g: (0, 0, 0)),
        ],
        out_specs=(pl.BlockSpec((c, vw), lambda b, g: (fwd(b, g), 0)), pl.BlockSpec((c, vw), lambda b, g: (bwd(b, g), 0))),
        scratch_shapes=[pltpu.VMEM((2, nh, GLA_DV, GLA_DK), F32)],
        compiler_params=_cparams(("arbitrary", "arbitrary")),
        name="gla_scan",
    )(p, p, p, paa, p, p, p, paa, wa_blk, ba_blk)


def _gla_out_kernel(of_ref, ob_ref, gate_ref, gn_ref, out_ref, *, nh):
    o = of_ref[...] + ob_ref[...]
    for h in range(nh):
        vs = slice(h * GLA_DV, (h + 1) * GLA_DV)
        gate = gate_ref[:, vs].astype(F32)
        out_ref[:, vs] = (_rms(o[:, vs], gn_ref[...]) * _silu(gate)).astype(out_ref.dtype)


def gla_output(o_f, o_b, p, gn, in_tile, *, gate_block, nh):
    vw = nh * GLA_DV
    row = pl.BlockSpec((ROW_TILE, vw), lambda i: (in_tile(i), 0))
    return pl.pallas_call(
        functools.partial(_gla_out_kernel, nh=nh),
        out_shape=jax.ShapeDtypeStruct((in_tile.n * ROW_TILE, vw), BF16),
        grid=(in_tile.n,),
        in_specs=[row, row, pl.BlockSpec((ROW_TILE, vw), lambda i: (in_tile(i), gate_block)),
                  pl.BlockSpec((1, GLA_DV), lambda i: (0, 0))],
        out_specs=pl.BlockSpec((ROW_TILE, vw), lambda i: (i, 0)),
        compiler_params=_cparams(("arbitrary",)),
        name="gla_output",
    )(o_f, o_b, p, gn.reshape(1, GLA_DV))


def _rope(x, cos, sin_signed):
    lane = lax.broadcasted_iota(jnp.int32, x.shape, 1)
    first = (lane % (HEAD_DIM // 2)) < (HEAD_DIM // 4)
    rot = jnp.where(first, pltpu.roll(x, HEAD_DIM - HEAD_DIM // 4, 1), pltpu.roll(x, HEAD_DIM // 4, 1))
    return x * cos + rot * sin_signed


def rope_tables(n_ctx, seq):
    nf = HEAD_DIM // 4
    rows = seq // GRID_W
    row = jnp.repeat(jnp.arange(rows, dtype=jnp.int32), GRID_W).astype(F32)
    col = jnp.tile(jnp.arange(GRID_W, dtype=jnp.int32), rows).astype(F32)
    inv = ROPE_THETA ** (-jnp.arange(nf, dtype=F32) / nf)
    ang = jnp.concatenate([row[:, None] * inv, row[:, None] * inv, col[:, None] * inv, col[:, None] * inv], axis=1)
    sign = jnp.tile(jnp.concatenate([-jnp.ones((nf,), F32), jnp.ones((nf,), F32)]), 2)
    cos = jnp.concatenate([jnp.ones((n_ctx, HEAD_DIM), F32), jnp.cos(ang)], axis=0)
    sin = jnp.concatenate([jnp.zeros((n_ctx, HEAD_DIM), F32), jnp.sin(ang) * sign], axis=0)
    return cos, sin


def _attend(q, keys, vals):
    s = _dot_nt(q, keys)
    e = jnp.exp(s - jnp.max(s, axis=-1, keepdims=True))
    return _dot(e.astype(BF16), vals), jnp.sum(e, axis=-1, keepdims=True)


def _gqa_kernel(q_ref, k_ref, v_ref, cq_ref, sq_ref, ck_ref, sk_ref, qn_ref, kn_ref, o_ref, ks_ref, *,
                n_ctx, ctx_tile_first):
    qi = pl.program_id(2)

    n_kv = k_ref.shape[1] // HEAD_DIM

    @pl.when(qi == 0)
    def _():
        for j in range(n_kv):
            cs = slice(j * HEAD_DIM, (j + 1) * HEAD_DIM)
            k = _rms(k_ref[:, cs].astype(F32), kn_ref[...])
            ks_ref[:, cs] = _rope(k, ck_ref[...], sk_ref[...]).astype(BF16)

    def attend(nk):
        for j in range(n_kv):
            cs = slice(j * HEAD_DIM, (j + 1) * HEAD_DIM)
            vals = v_ref[0:nk, cs]
            for g in range(GQA_GROUP):
                head = j * GQA_GROUP + g
                hs = slice(head * HEAD_DIM, (head + 1) * HEAD_DIM)
                q = _rms(q_ref[:, hs].astype(F32), qn_ref[...])
                q = (_rope(q, cq_ref[...], sq_ref[...]) * (HEAD_DIM ** -0.5)).astype(BF16)
                pv, l = _attend(q, ks_ref[0:nk, cs], vals)
                o_ref[:, hs] = (pv / l).astype(o_ref.dtype)

    if ctx_tile_first:
        @pl.when(qi == 0)
        def _():
            attend(n_ctx)

        @pl.when(qi > 0)
        def _():
            attend(k_ref.shape[0])
    else:
        attend(k_ref.shape[0])


def gqa_attention(p, cos, sin, qn, kn, *, nb, rows_per_batch, n_ctx, n_kv, q_col, k_col, v_col, with_ctx):
    t = p.shape[0]
    tpb = rows_per_batch // ROW_TILE
    nq = tpb if with_ctx else tpb - 1
    off = 0 if with_ctx else 1
    kvw = GQA_KV_PER_STEP * HEAD_DIM
    qw = GQA_GROUP * kvw
    return pl.pallas_call(
        functools.partial(_gqa_kernel, n_ctx=n_ctx, ctx_tile_first=with_ctx),
        out_shape=jax.ShapeDtypeStruct((nb * nq * ROW_TILE, n_kv * GQA_GROUP * HEAD_DIM), BF16),
        grid=(nb, n_kv // GQA_KV_PER_STEP, nq),
        in_specs=[
            pl.BlockSpec((ROW_TILE, qw), lambda b, h, i: (b * tpb + i + off, q_col // qw + h)),
            pl.BlockSpec((rows_per_batch, kvw), lambda b, h, i: (b, k_col // kvw + h)),
            pl.BlockSpec((rows_per_batch, kvw), lambda b, h, i: (b, v_col // kvw + h)),
            pl.BlockSpec((ROW_TILE, HEAD_DIM), lambda b, h, i: (i + off, 0)),
            pl.BlockSpec((ROW_TILE, HEAD_DIM), lambda b, h, i: (i + off, 0)),
            pl.BlockSpec((rows_per_batch, HEAD_DIM), lambda b, h, i: (0, 0)),
            pl.BlockSpec((rows_per_batch, HEAD_DIM), lambda b, h, i: (0, 0)),
            pl.BlockSpec((1, HEAD_DIM), lambda b, h, i: (0, 0)),
            pl.BlockSpec((1, HEAD_DIM), lambda b, h, i: (0, 0)),
        ],
        out_specs=pl.BlockSpec((ROW_TILE, qw), lambda b, h, i: (b * nq + i, h)),
        scratch_shapes=[pltpu.VMEM((rows_per_batch, kvw), BF16)],
        compiler_params=_cparams(("arbitrary", "arbitrary", "arbitrary")),
        name="gqa_attention",
    )(p, p, p, cos, sin, cos, sin, qn.reshape(1, HEAD_DIM), kn.reshape(1, HEAD_DIM))


def _diff_kernel(q_ref, k_ref, v_ref, cq_ref, sq_ref, ck_ref, sk_ref, lq1_ref, lk1_ref, lq2_ref, lk2_ref, gn_ref,
                 o_ref, ks_ref, *, n_ctx, ctx_tile_first, lam_init):
    qi = pl.program_id(1)
    nh = v_ref.shape[1] // DIFF_DV

    @pl.when(qi == 0)
    def _():
        for j in range(2 * nh):
            hs = slice(j * HEAD_DIM, (j + 1) * HEAD_DIM)
            ks_ref[:, hs] = _rope(k_ref[:, hs].astype(F32), ck_ref[...], sk_ref[...]).astype(BF16)

    lam = (jnp.exp(jnp.sum(lq1_ref[...] * lk1_ref[...], axis=-1, keepdims=True))
           - jnp.exp(jnp.sum(lq2_ref[...] * lk2_ref[...], axis=-1, keepdims=True)) + lam_init)

    def attend(nk):
        for h in range(nh):
            vs = slice(h * DIFF_DV, (h + 1) * DIFF_DV)
            vals = v_ref[0:nk, vs]
            o = None
            for j in range(2):
                hs = slice((2 * h + j) * HEAD_DIM, (2 * h + j + 1) * HEAD_DIM)
                q = (_rope(q_ref[:, hs].astype(F32), cq_ref[...], sq_ref[...]) * (HEAD_DIM ** -0.5)).astype(BF16)
                pv, l = _attend(q, ks_ref[0:nk, hs], vals)
                o = pv / l if j == 0 else o - (lam / l) * pv
            o_ref[:, vs] = (_rms(o, gn_ref[...]) * (1.0 - lam_init)).astype(o_ref.dtype)

    if ctx_tile_first:
        @pl.when(qi == 0)
        def _():
            attend(n_ctx)

        @pl.when(qi > 0)
        def _():
            attend(k_ref.shape[0])
    else:
        attend(k_ref.shape[0])


def diff_attention(p, cos, sin, lq1, lk1, lq2, lk2, gn, *, nb, rows_per_batch, n_ctx, nh, q_col, k_col, v_col,
                   with_ctx, lam_init):
    t = p.shape[0]
    tpb = rows_per_batch // ROW_TILE
    nq = tpb if with_ctx else tpb - 1
    off = 0 if with_ctx else 1
    w = nh * DIFF_DV
    vec = pl.BlockSpec((1, HEAD_DIM), lambda b, i: (0, 0))
    return pl.pallas_call(
        functools.partial(_diff_kernel, n_ctx=n_ctx, ctx_tile_first=with_ctx, lam_init=lam_init),
        out_shape=jax.ShapeDtypeStruct((nb * nq * ROW_TILE, w), BF16),
        grid=(nb, nq),
        in_specs=[
            pl.BlockSpec((ROW_TILE, w), lambda b, i: (b * tpb + i + off, q_col // w)),
            pl.BlockSpec((rows_per_batch, w), lambda b, i: (b, k_col // w)),
            pl.BlockSpec((rows_per_batch, w), lambda b, i: (b, v_col // w)),
            pl.BlockSpec((ROW_TILE, HEAD_DIM), lambda b, i: (i + off, 0)),
            pl.BlockSpec((ROW_TILE, HEAD_DIM), lambda b, i: (i + off, 0)),
            pl.BlockSpec((rows_per_batch, HEAD_DIM), lambda b, i: (0, 0)),
            pl.BlockSpec((rows_per_batch, HEAD_DIM), lambda b, i: (0, 0)),
            vec, vec, vec, vec,
            pl.BlockSpec((1, DIFF_DV), lambda b, i: (0, 0)),
        ],
        out_specs=pl.BlockSpec((ROW_TILE, w), lambda b, i: (b * nq + i, 0)),
        scratch_shapes=[pltpu.VMEM((rows_per_batch, w), BF16)],
        compiler_params=_cparams(("arbitrary", "arbitrary")),
        name="diff_attention",
    )(p, p, p, cos, sin, cos, sin, lq1.reshape(1, -1), lk1.reshape(1, -1), lq2.reshape(1, -1), lk2.reshape(1, -1),
      gn.reshape(1, DIFF_DV))


def _route_kernel(lg_ref, bias_ref, o_ref):
    x = lg_ref[...] + bias_ref[...]
    lane = lax.broadcasted_iota(jnp.int32, x.shape, 1).astype(F32)
    neg = -jnp.inf

    def first_max(vals, mask):
        v = jnp.where(mask, vals, neg)
        m = jnp.max(v, axis=-1, keepdims=True)
        idx = jnp.min(jnp.where(mask & (v == m), lane, float(LANES)), axis=-1, keepdims=True)
        return m, idx

    gmask = lane < N_GROUPS
    gm, gidx = first_max(x, gmask)
    g_w = 1.0 / jnp.sum(jnp.where(gmask, jnp.exp(x - gm), 0.0), axis=-1, keepdims=True)
    lo = N_GROUPS + EXPERTS_PER_GROUP * gidx
    emask = (lane >= lo) & (lane < lo + EXPERTS_PER_GROUP)
    m1, i1 = first_max(x, emask)
    m2, i2 = first_max(x, emask & (lane != i1))
    r = jnp.exp(m2 - m1)
    w1 = g_w / (1.0 + r)
    w2 = g_w * r / (1.0 + r)
    out = jnp.where(lane == 0, i1 - N_GROUPS, 0.0)
    out = jnp.where(lane == 1, i2 - N_GROUPS, out)
    out = jnp.where(lane == 2, w1, out)
    out = jnp.where(lane == 3, w2, out)
    o_ref[...] = out


def route(logits, bias):
    t = logits.shape[0]
    return pl.pallas_call(
        _route_kernel,
        out_shape=jax.ShapeDtypeStruct((t, LANES), F32),
        grid=(t // ROW_TILE,),
        in_specs=[pl.BlockSpec((ROW_TILE, LANES), lambda i: (i, 0)), pl.BlockSpec((1, LANES), lambda i: (0, 0))],
        out_specs=pl.BlockSpec((ROW_TILE, LANES), lambda i: (i, 0)),
        compiler_params=_cparams(("arbitrary",)),
        name="route",
    )(logits, bias)


def dispatch_plan(ids):
    t = ids.shape[0]
    tps = SEG_ROWS // SEG_TILE
    n_src = 2 * t + N_EXPERTS * SEG_TILE
    n_seg = (n_src // SEG_TILE + (tps - 1) * N_EXPERTS) // tps + 1 + (N_EXPERTS * SEG_TILE) // SEG_ROWS + 2
    e = ids.reshape(-1)
    onehot = (e[:, None] == jnp.arange(N_EXPERTS, dtype=jnp.int32)[None, :]).astype(jnp.int32)
    csum = jnp.cumsum(onehot, axis=0)
    counts = csum[-1]
    tiles = (counts + SEG_TILE - 1) // SEG_TILE
    segs = (tiles + tps - 1) // tps
    seg_end = jnp.cumsum(segs)
    seg_start = seg_end - segs
    row_start = (jnp.cumsum(tiles) - tiles) * SEG_TILE
    rank = csum - 1
    cpos = jnp.sum(onehot * (rank + row_start[None, :]), axis=1)
    src = jnp.zeros((n_src,), jnp.int32).at[cpos].set(jnp.arange(2 * t, dtype=jnp.int32) // 2)
    sidx = jnp.arange(n_seg, dtype=jnp.int32)
    seg_expert = jnp.minimum(jnp.sum((seg_end[None, :] <= sidx[:, None]).astype(jnp.int32), axis=1), N_EXPERTS - 1)
    sel = (seg_expert[:, None] == jnp.arange(N_EXPERTS, dtype=jnp.int32)[None, :]).astype(jnp.int32)
    k = sidx - jnp.sum(sel * seg_start[None, :], axis=1)
    seg_row0 = jnp.sum(sel * row_start[None, :], axis=1) + k * SEG_ROWS
    used = sidx < seg_end[-1]
    seg_nt = jnp.where(used, jnp.clip(jnp.sum(sel * tiles[None, :], axis=1) - k * tps, 0, tps), 0)
    y_rows = n_src + SEG_ROWS
    fill0 = jnp.sum(tiles) * SEG_TILE + (sidx - seg_end[-1]) * SEG_ROWS
    seg_row0 = jnp.where(used, seg_row0, jnp.minimum(fill0, y_rows - SEG_ROWS))
    seg_fill = jnp.where(used, 0, (fill0 < y_rows).astype(jnp.int32))
    return src, cpos.reshape(t, 2), seg_expert, seg_row0, seg_nt, seg_fill, seg_end[-1:]


def _row_copy(src_hbm, idx, dst_vmem, r, sem):
    return pltpu.make_async_copy(src_hbm.at[pl.ds(idx, 1)], dst_vmem.at[pl.ds(r, 1)], sem)


def _expert_kernel(se_ref, r0_ref, nt_ref, fill_ref, nseg_ref, src_ref, h_hbm, wg_hbm, wu_hbm, wd_hbm, y_hbm,
                   land, xs, acc_a, acc_u, hid, st_g, st_u, st_d, wb_g, wb_u, wd_res, yst,
                   sem_x, sem_g, sem_u, sem_d, sem_y, *, expert_base):
    s = pl.program_id(0)
    n_seg = nseg_ref[0]
    d = xs.shape[1]
    ck = st_g.shape[1]
    cd = st_d.shape[1]
    n_k = d // ck
    n_c = wd_res.shape[0] // cd
    assert n_k > W_SLOTS and n_c == n_k

    def start_gather(seg):
        r0 = r0_ref[seg]

        def tile(tt, carry):
            base = tt * SEG_TILE

            def body(r, c):
                _row_copy(h_hbm, src_ref[r0 + base + r], land, base + r, sem_x).start()
                return c

            return lax.fori_loop(0, SEG_TILE, body, carry, unroll=8)

        lax.fori_loop(0, nt_ref[seg], tile, 0)

    def finish_gather(seg):
        nt = nt_ref[seg]

        def wait(t, carry):
            pltpu.make_async_copy(h_hbm.at[pl.ds(0, SEG_TILE)], land.at[pl.ds(0, SEG_TILE)], sem_x).wait()
            return carry

        lax.fori_loop(0, nt, wait, 0)

        def convert(t, carry):
            rows = pl.ds(pl.multiple_of(t * SEG_TILE, SEG_TILE), SEG_TILE)
            xs[rows, :] = land[rows, :].astype(BF16)
            return carry

        lax.fori_loop(0, nt, convert, 0)

    def in_copies(e, kc):
        slot = kc % W_SLOTS
        rows = pl.ds(pl.multiple_of(kc * ck, ck), ck)
        return (pltpu.make_async_copy(wg_hbm.at[e, rows], st_g.at[slot], sem_g.at[slot]),
                pltpu.make_async_copy(wu_hbm.at[e, rows], st_u.at[slot], sem_u.at[slot]))

    def out_copy(e, c):
        slot = c % W_SLOTS
        rows = pl.ds(pl.multiple_of(c * cd, cd), cd)
        return pltpu.make_async_copy(wd_hbm.at[e, rows], st_d.at[slot], sem_d.at[slot])

    def y_copy(t, slot=None):
        slot = t % 2 if slot is None else slot
        rows = pl.ds(pl.multiple_of(r0_ref[s] + t * SEG_TILE, SEG_TILE), SEG_TILE)
        return pltpu.make_async_copy(yst.at[slot], y_hbm.at[rows], sem_y.at[slot])

    def land_in(e, kc):
        for cp in in_copies(e, kc):
            cp.wait()
        wb_g[kc % 2] = st_g[kc % W_SLOTS].astype(BF16)
        wb_u[kc % 2] = st_u[kc % W_SLOTS].astype(BF16)

        @pl.when(kc + W_SLOTS < n_k)
        def _():
            for cp in in_copies(e, kc + W_SLOTS):
                cp.start()

    def land_out(e, c):
        out_copy(e, c).wait()
        wd_res[pl.ds(pl.multiple_of(c * cd, cd), cd), :] = st_d[c % W_SLOTS].astype(BF16)

        @pl.when(c + W_SLOTS < n_c)
        def _():
            out_copy(e, c + W_SLOTS).start()

    def gate_up(kc):
        x_k = xs[:, pl.ds(pl.multiple_of(kc * ck, ck), ck)]
        acc_a[...] += _dot(x_k, wb_g[kc % 2])
        acc_u[...] += _dot(x_k, wb_u[kc % 2])

    def down(t):
        rows = pl.ds(pl.multiple_of(t * SEG_TILE, SEG_TILE), SEG_TILE)
        yst[t % 2] = _dot(hid[rows, :], wd_res[...])
        y_copy(t).start()

    @pl.when(s == 0)
    def _():
        xs[...] = jnp.zeros_like(xs)
        start_gather(0)

    def start_first_in(seg):
        for i in range(W_SLOTS):
            for cp in in_copies(expert_base + se_ref[seg], i):
                cp.start()

    def start_first_out(seg):
        for i in range(W_SLOTS):
            out_copy(expert_base + se_ref[seg], i).start()

    @pl.when(s == 0)
    def _():
        start_first_in(0)
        start_first_out(0)

    @pl.when(s < n_seg)
    def _():
        e = expert_base + se_ref[s]
        finish_gather(s)

        @pl.when(s + 1 < n_seg)
        def _():
            start_gather(s + 1)

        acc_a[...] = jnp.zeros_like(acc_a)
        acc_u[...] = jnp.zeros_like(acc_u)
        land_in(e, 0)

        def in_step(kc, carry):
            land_in(e, kc + 1)
            land_out(e, kc)
            gate_up(kc)
            return carry

        lax.fori_loop(0, n_k - 1, in_step, 0)

        @pl.when(s + 1 < n_seg)
        def _():
            start_first_in(s + 1)

        land_out(e, n_k - 1)

        @pl.when(s + 1 < n_seg)
        def _():
            start_first_out(s + 1)

        gate_up(n_k - 1)
        hid[...] = (_silu(acc_a[...]) * acc_u[...]).astype(BF16)
        nt = nt_ref[s]

        def out_step(t, carry):
            @pl.when(t >= 2)
            def _():
                y_copy(t - 2).wait()

            down(t)
            return carry

        lax.fori_loop(0, nt, out_step, 0)

        @pl.when(nt >= 2)
        def _():
            y_copy(nt - 2).wait()

        y_copy(nt - 1).wait()

    @pl.when(fill_ref[s] == 1)
    def _():
        yst[0] = jnp.zeros_like(yst[0])
        for t in range(SEG_ROWS // SEG_TILE):
            y_copy(t, 0).start()
        for t in range(SEG_ROWS // SEG_TILE):
            y_copy(t, 0).wait()


def expert_mlp(h, plan, wg, wu, wd, *, expert_base):
    src, _, seg_expert, seg_row0, seg_nt, seg_fill, n_seg = plan
    d = h.shape[1]
    f = wg.shape[2]
    n_grid = seg_expert.shape[0]
    ck = d // W_CHUNKS
    cn = d // W_CHUNKS
    return pl.pallas_call(
        functools.partial(_expert_kernel, expert_base=expert_base),
        out_shape=jax.ShapeDtypeStruct((src.shape[0] + SEG_ROWS, d), F32),
        grid_spec=pltpu.PrefetchScalarGridSpec(
            num_scalar_prefetch=6,
            grid=(n_grid,),
            in_specs=[pl.BlockSpec(memory_space=pl.ANY)] * 4,
            out_specs=pl.BlockSpec(memory_space=pl.ANY),
            scratch_shapes=[
                pltpu.VMEM((SEG_ROWS, d), F32), pltpu.VMEM((SEG_ROWS, d), BF16),
                pltpu.VMEM((SEG_ROWS, f), F32), pltpu.VMEM((SEG_ROWS, f), F32), pltpu.VMEM((SEG_ROWS, f), BF16),
                pltpu.VMEM((W_SLOTS, ck, f), F32), pltpu.VMEM((W_SLOTS, ck, f), F32),
                pltpu.VMEM((W_SLOTS, f // W_CHUNKS, d), F32),
                pltpu.VMEM((2, ck, f), BF16), pltpu.VMEM((2, ck, f), BF16), pltpu.VMEM((f, d), BF16),
                pltpu.VMEM((2, SEG_TILE, d), F32),
                pltpu.SemaphoreType.DMA, pltpu.SemaphoreType.DMA((W_SLOTS,)), pltpu.SemaphoreType.DMA((W_SLOTS,)),
                pltpu.SemaphoreType.DMA((W_SLOTS,)), pltpu.SemaphoreType.DMA((2,)),
            ],
        ),
        compiler_params=_cparams(("arbitrary",)),
        name="moe_experts",
    )(seg_expert, seg_row0, seg_nt, seg_fill, n_seg, src, h, wg, wu, wd)


def _combine_kernel(p1_ref, p2_ref, x_ref, g_ref, r_ref, fg_ref, y_hbm, o_ref, buf_ref, sem, *, final_norm):
    i = pl.program_id(0)
    rows = x_ref.shape[0]
    slot = i % 2

    def start_tile(tile, sl):
        def start(r, carry):
            _row_copy(y_hbm, p1_ref[tile * rows + r], buf_ref.at[sl, 0], r, sem.at[sl]).start()
            _row_copy(y_hbm, p2_ref[tile * rows + r], buf_ref.at[sl, 1], r, sem.at[sl]).start()
            return carry

        lax.fori_loop(0, rows, start, 0, unroll=8)

    @pl.when(i == 0)
    def _():
        start_tile(0, 0)

    @pl.when(i + 1 < pl.num_programs(0))
    def _():
        start_tile(i + 1, 1 - slot)

    pltpu.make_async_copy(y_hbm.at[pl.ds(0, rows)], buf_ref.at[slot, 0], sem.at[slot]).wait()
    pltpu.make_async_copy(y_hbm.at[pl.ds(0, rows)], buf_ref.at[slot, 1], sem.at[slot]).wait()
    w1 = r_ref[:, 2:3]
    w2 = r_ref[:, 3:4]
    out = x_ref[...] + g_ref[0] * (w1 * buf_ref[slot, 0] + w2 * buf_ref[slot, 1])
    o_ref[...] = _rms(out, fg_ref[...]) if final_norm else out


def moe_combine(x, y, pos, routed, mod3, gate_chunk, mod_row, final_g=None):
    t, d = x.shape
    fg = jnp.ones((1, d), F32) if final_g is None else final_g.reshape(1, d)
    return pl.pallas_call(
        functools.partial(_combine_kernel, final_norm=final_g is not None),
        out_shape=jax.ShapeDtypeStruct((t, d), F32),
        grid_spec=pltpu.PrefetchScalarGridSpec(
            num_scalar_prefetch=2,
            grid=(t // ROW_TILE,),
            in_specs=[
                pl.BlockSpec((ROW_TILE, d), lambda i, a, b: (i, 0)),
                pl.BlockSpec((1, 1, d), lambda i, a, b: (mod_row(i), 0, gate_chunk)),
                pl.BlockSpec((ROW_TILE, LANES), lambda i, a, b: (i, 0)),
                pl.BlockSpec((1, d), lambda i, a, b: (0, 0)),
                pl.BlockSpec(memory_space=pl.ANY),
            ],
            out_specs=pl.BlockSpec((ROW_TILE, d), lambda i, a, b: (i, 0)),
            scratch_shapes=[pltpu.VMEM((2, 2, ROW_TILE, d), F32), pltpu.SemaphoreType.DMA((2,))],
        ),
        compiler_params=_cparams(("arbitrary",)),
        name="moe_combine",
    )(pos[:, 0], pos[:, 1], x, mod3, routed, fg, y)


def kernel(x, c, ctx, c_ctx, ada_w, ada_b, norm1_g, w_in, gla_wa_fwd, gla_ba_fwd, gla_wa_bwd, gla_ba_bwd, gla_norm_g,
           gqa_qnorm_g, gqa_knorm_g, diff_lq1, diff_lk1, diff_lq2, diff_lk2, diff_norm_g, w_out, norm2_g, router_wg,
           router_bg, router_we, router_be, moe_w_gate, moe_w_up, moe_w_down, final_norm_g):
    nb, seq, d = x.shape
    n_ctx = ctx.shape[1]
    depth = ada_w.shape[0]
    assert n_ctx == ROW_TILE and seq % ROW_TILE == 0 and seq % GRID_W == 0
    rpb = n_ctx + seq
    tpb = rpb // ROW_TILE
    t = nb * rpb

    gla_heads = (d // 4) // GLA_DV
    gqa_heads = (d // 2) // HEAD_DIM
    n_kv = gqa_heads // GQA_GROUP
    diff_heads = (d // 4) // DIFF_DV
    kw = gla_heads * GLA_DK
    vw = gla_heads * GLA_DV
    splits = (kw, kw, vw, vw, GLA_RANK, GLA_RANK, gqa_heads * HEAD_DIM, n_kv * HEAD_DIM, n_kv * HEAD_DIM,
              diff_heads * 2 * HEAD_DIM, diff_heads * 2 * HEAD_DIM, diff_heads * DIFF_DV)
    offs = [0]
    for s in splits:
        offs.append(offs[-1] + s)
    col = {}
    acc = 0
    for name, width in (("aq", kw), ("ak", kw), ("av", vw), ("ag", vw), ("bq", splits[6]), ("bk", splits[7]),
                        ("bv", splits[8]), ("dq", splits[9]), ("dk", splits[10]), ("dv", splits[11])):
        col[name] = acc
        acc += width

    def mod_row_full(i):
        return jnp.where(i % tpb == 0, nb, i // tpb)

    def mod_row_latent(i):
        return i // (tpb - 1)

    tokens = jnp.concatenate([ctx, x], axis=1).reshape(t, d)
    c_all = jnp.zeros((8, d), F32).at[:nb].set(c).at[nb].set(c_ctx)
    mod = ada_modulation(c_all, ada_w, ada_b)
    cos, sin = rope_tables(n_ctx, seq)
    w_wide, w_dec_all = split_rows_transpose(jnp.swapaxes(w_in, 1, 2), offs[4], offs[6])
    w_out_bf = w_out.astype(BF16)
    f = moe_w_gate.shape[-1]
    wg_all = moe_w_gate.reshape(depth * N_EXPERTS, d, f)
    wu_all = moe_w_up.reshape(depth * N_EXPERTS, d, f)
    wd_all = moe_w_down.reshape(depth * N_EXPERTS, f, d)

    for l in range(depth):
        last = l == depth - 1
        lam_init = 0.8 - 0.6 * math.exp(-0.3 * l)
        mod3 = mod[l].reshape(8, 1, 6 * d)

        h, paa = norm_modulate(tokens, norm1_g[l], mod3, 1, 0, w_dec_all[l], mod_row_full, h_dtype=BF16,
                               exact_small=False)
        p = matmul(h, w_wide, l, bm=1024, bn=1024, out_dtype=BF16)

        wa_blk = jnp.zeros((2, LANES, kw), F32)
        wa_blk = wa_blk.at[0, :GLA_RANK].set(gla_wa_fwd[l]).at[1, GLA_RANK:2 * GLA_RANK].set(gla_wa_bwd[l])
        ba_blk = jnp.stack([gla_ba_fwd[l], gla_ba_bwd[l]]).reshape(2, 1, kw)
        o_f, o_b = gla_scan(p, paa, wa_blk, ba_blk, nb=nb, rows_per_batch=rpb, n_ctx_chunks=n_ctx // GLA_CHUNK,
                            nh=gla_heads)
        if last:
            x_tile, mod_row = _LatentTiles(nb, tpb), mod_row_latent
        else:
            x_tile, mod_row = _AllTiles(t // ROW_TILE), mod_row_full
        mix_a = gla_output(o_f, o_b, p, gla_norm_g[l], x_tile, gate_block=col["ag"] // vw, nh=gla_heads)
        mix_b = gqa_attention(p, cos, sin, gqa_qnorm_g[l], gqa_knorm_g[l], nb=nb, rows_per_batch=rpb, n_ctx=n_ctx,
                              n_kv=n_kv, q_col=col["bq"], k_col=col["bk"], v_col=col["bv"], with_ctx=not last)
        mix_c = diff_attention(p, cos, sin, diff_lq1[l], diff_lk1[l], diff_lq2[l], diff_lk2[l], diff_norm_g[l],
                               nb=nb, rows_per_batch=rpb, n_ctx=n_ctx, nh=diff_heads, q_col=col["dq"],
                               k_col=col["dk"], v_col=col["dv"], with_ctx=not last, lam_init=lam_init)
        tokens = matmul3_gated_residual(mix_a, mix_b, mix_c, w_out_bf, l, tokens, mod3, 2, x_tile, mod_row, bn=1024)

        w_route = jnp.zeros((d, LANES), F32).at[:, :N_GROUPS].set(router_wg[l])
        w_route = w_route.at[:, N_GROUPS:N_GROUPS + N_EXPERTS].set(router_we[l])
        b_route = jnp.zeros((1, LANES), F32).at[0, :N_GROUPS].set(router_bg[l])
        b_route = b_route.at[0, N_GROUPS:N_GROUPS + N_EXPERTS].set(router_be[l])
        h2, logits = norm_modulate(tokens, norm2_g[l], mod3, 4, 3, w_route, mod_row, h_dtype=F32, exact_small=True)
        routed = route(logits, b_route)
        plan = dispatch_plan(routed[:, 0:2].astype(jnp.int32))
        y = expert_mlp(h2, plan, wg_all, wu_all, wd_all, expert_base=l * N_EXPERTS)
        tokens = moe_combine(tokens, y, plan[1], routed, mod3, 5, mod_row, final_norm_g if last else None)

    return tokens.reshape(nb, seq, d)
```
